```python
import jax, jax.numpy as jnp
from jax import lax
import numpy as np


D_MODEL = 1024
BATCH = 4
SEQ = 8192
DEPTH = 4
DEC_BATCH = 2
DEC_SEQ = 16384
PAST_LEN = 128

N_MEM = 256
XA_HEADS = 4
XA_HEAD_DIM = 128
XA_WIDTH = XA_HEADS * XA_HEAD_DIM
CHUNK = 128
SGU_WIDTH = 1536
SGU_GROUPS = 8
SGU_GROUP_DIM = SGU_WIDTH // SGU_GROUPS
MLA_HEADS = 8
Q_LORA = 256
KV_LORA = 128
QK_NOPE = 128
QK_ROPE = 64
V_HEAD = 128
ROPE_THETA = 10000.0
Q_BLOCK = 128
D_FF = 2816
N_SGU_LAYERS = (DEPTH + 1) // 2
N_MLA_LAYERS = DEPTH // 2
NORM_EPS = 1e-6

kernel_name = 'hybrid_sgu_mla_macaron_encoder'


def rms_norm(x, g):
    xf = x.astype(jnp.float32)
    y = xf * lax.rsqrt(jnp.mean(xf * xf, axis=-1, keepdims=True) + NORM_EPS)
    return (y * g.astype(jnp.float32)).astype(x.dtype)


def swiglu(h, w_in, w_out):
    gu = h @ w_in
    g, u = gu[..., :D_FF], gu[..., D_FF:]
    return (jax.nn.silu(g) * u) @ w_out


def rope_tables(seq_len):
    inv_freq = 1.0 / (ROPE_THETA ** (jnp.arange(0, QK_ROPE, 2, dtype=jnp.float32) / QK_ROPE))
    ang = jnp.arange(seq_len, dtype=jnp.float32)[:, None] * inv_freq[None, :]
    return jnp.cos(ang), jnp.sin(ang)


def apply_rope(x, cos, sin):
    half = x.shape[-1] // 2
    x1, x2 = x[..., :half], x[..., half:]
    c = cos.astype(x.dtype)
    s = sin.astype(x.dtype)
    return jnp.concatenate([x1 * c - x2 * s, x1 * s + x2 * c], axis=-1)


def memory_kv(mem, g, w):
    b, m, _ = mem.shape
    kv = rms_norm(mem, g) @ w
    k, v = kv[..., :XA_WIDTH], kv[..., XA_WIDTH:]
    return (k.reshape(b, m, XA_HEADS, XA_HEAD_DIM), v.reshape(b, m, XA_HEADS, XA_HEAD_DIM))


def memory_attention(q, mem_k, mem_v):
    b, s, _ = q.shape
    q = q.reshape(b, s, XA_HEADS, XA_HEAD_DIM) * (XA_HEAD_DIM ** -0.5)
    sc = jnp.einsum('bshd,bmhd->bhsm', q, mem_k, preferred_element_type=jnp.float32)
    p = jax.nn.softmax(sc, axis=-1).astype(mem_v.dtype)
    o = jnp.einsum('bhsm,bmhd->bshd', p, mem_v)
    return o.reshape(b, s, XA_WIDTH)


def sgu_mixer(h, mem_k, mem_v, w_in, v_norm, w_s, b_s, w_out):
    b, s, _ = h.shape
    proj = h @ w_in
    uv = jax.nn.gelu(proj[..., :2 * SGU_WIDTH])
    u, v = uv[..., :SGU_WIDTH], uv[..., SGU_WIDTH:]
    v = rms_norm(v, v_norm).reshape(b, s // CHUNK, CHUNK, SGU_GROUPS, SGU_GROUP_DIM)
    mixed = jnp.einsum('gpq,bnqgc->bnpgc', w_s, v) + b_s.T[None, None, :, :, None]
    gated = u * mixed.reshape(b, s, SGU_WIDTH)
    xa = memory_attention(proj[..., 2 * SGU_WIDTH:], mem_k, mem_v)
    return jnp.concatenate([gated, xa], axis=-1) @ w_out


def mla_mixer(h, mem_k, mem_v, cos, sin, w_in, q_norm, w_uq, kv_norm, w_uk, w_uv, w_out):
    b, s, _ = h.shape
    o1 = Q_LORA
    o2 = o1 + KV_LORA
    o3 = o2 + QK_ROPE
    proj = h @ w_in
    c_q = rms_norm(proj[..., :o1], q_norm)
    c_kv = rms_norm(proj[..., o1:o2], kv_norm)
    k_rope = apply_rope(proj[..., o2:o3], cos, sin)
    q = (c_q @ w_uq).reshape(b, s, MLA_HEADS, QK_NOPE + QK_ROPE)
    q_rope = apply_rope(q[..., QK_NOPE:], cos[:, None, :], sin[:, None, :])
    q_lat = jnp.einsum('bshn,chn->bshc', q[..., :QK_NOPE], w_uk)
    q_full = jnp.concatenate([q_lat, q_rope], axis=-1) * ((QK_NOPE + QK_ROPE) ** -0.5)
    k_full = jnp.concatenate([c_kv, k_rope], axis=-1)
    dk = KV_LORA + QK_ROPE
    q_blocks = q_full.reshape(b, s // Q_BLOCK, Q_BLOCK, MLA_HEADS, dk).transpose(1, 0, 2, 3, 4)

    def attend(qb):
        sc = jnp.einsum('bqhd,bkd->bhqk', qb, k_full, preferred_element_type=jnp.float32)
        p = jax.nn.softmax(sc, axis=-1).astype(c_kv.dtype)
        return jnp.einsum('bhqk,bkc->bqhc', p, c_kv)

    o_lat = lax.map(attend, q_blocks).transpose(1, 0, 2, 3, 4).reshape(b, s, MLA_HEADS, KV_LORA)
    o = jnp.einsum('bshc,chv->bshv', o_lat, w_uv).reshape(b, s, MLA_HEADS * V_HEAD)
    xa = memory_attention(proj[..., o3:], mem_k, mem_v)
    return jnp.concatenate([o, xa], axis=-1) @ w_out


def trunk(x, mem, p):
    cos, sin = rope_tables(x.shape[1])
    for i in range(DEPTH):
        j = i // 2
        x = x + 0.5 * swiglu(rms_norm(x, p['ffn1_norm'][i]), p['ffn1_w_in'][i], p['ffn1_w_out'][i])
        mem_k, mem_v = memory_kv(mem, p['mem_norm'][i], p['w_mem_kv'][i])
        h = rms_norm(x, p['mix_norm'][i])
        if i % 2 == 0:
            x = x + sgu_mixer(h, mem_k, mem_v, p['sgu_w_in'][j], p['sgu_v_norm'][j],
                              p['sgu_w_s'][j], p['sgu_b_s'][j], p['sgu_w_out'][j])
        else:
            x = x + mla_mixer(h, mem_k, mem_v, cos, sin, p['mla_w_in'][j], p['mla_q_norm'][j],
                              p['mla_w_uq'][j], p['mla_kv_norm'][j], p['mla_w_uk'][j],
                              p['mla_w_uv'][j], p['mla_w_out'][j])
        x = x + 0.5 * swiglu(rms_norm(x, p['ffn2_norm'][i]), p['ffn2_w_in'][i], p['ffn2_w_out'][i])
    return rms_norm(x, p['final_norm'])


def setup_inputs(seed: int = 0) -> dict:
    key = jax.random.key(seed)
    ks = jax.random.split(key, 32)

    def nrm(k, shape, scale):
        return jax.random.normal(k, shape, jnp.float32) * scale

    def gain(k, shape):
        return 1.0 + 0.02 * jax.random.normal(k, shape, jnp.float32)

    sgu_in = 2 * SGU_WIDTH + XA_WIDTH
    mla_in = Q_LORA + KV_LORA + QK_ROPE + XA_WIDTH
    return {
        'x_prompt': nrm(ks[0], (BATCH, SEQ, D_MODEL), 1.0),
        'x_sample': nrm(ks[1], (DEC_BATCH, DEC_SEQ, D_MODEL), 1.0),
        'mem_prompt': nrm(ks[2], (BATCH, N_MEM, D_MODEL), 1.0),
        'mem_sample': nrm(ks[3], (DEC_BATCH, N_MEM, D_MODEL), 1.0),
        'ffn1_norm': gain(ks[4], (DEPTH, D_MODEL)),
        'ffn1_w_in': nrm(ks[5], (DEPTH, D_MODEL, 2 * D_FF), D_MODEL ** -0.5),
        'ffn1_w_out': nrm(ks[6], (DEPTH, D_FF, D_MODEL), D_FF ** -0.5),
        'mix_norm': gain(ks[7], (DEPTH, D_MODEL)),
        'mem_norm': gain(ks[8], (DEPTH, D_MODEL)),
        'w_mem_kv': nrm(ks[9], (DEPTH, D_MODEL, 2 * XA_WIDTH), D_MODEL ** -0.5),
        'ffn2_norm': gain(ks[10], (DEPTH, D_MODEL)),
        'ffn2_w_in': nrm(ks[11], (DEPTH, D_MODEL, 2 * D_FF), D_MODEL ** -0.5),
        'ffn2_w_out': nrm(ks[12], (DEPTH, D_FF, D_MODEL), D_FF ** -0.5),
        'sgu_w_in': nrm(ks[13], (N_SGU_LAYERS, D_MODEL, sgu_in), D_MODEL ** -0.5),
        'sgu_v_norm': gain(ks[14], (N_SGU_LAYERS, SGU_WIDTH)),
        'sgu_w_s': nrm(ks[15], (N_SGU_LAYERS, SGU_GROUPS, CHUNK, CHUNK), 0.5 * CHUNK ** -0.5),
        'sgu_b_s': 1.0 + nrm(ks[16], (N_SGU_LAYERS, SGU_GROUPS, CHUNK), 0.02),
        'sgu_w_out': nrm(ks[17], (N_SGU_LAYERS, SGU_WIDTH + XA_WIDTH, D_MODEL), (SGU_WIDTH + XA_WIDTH) ** -0.5),
        'mla_w_in': nrm(ks[18], (N_MLA_LAYERS, D_MODEL, mla_in), D_MODEL ** -0.5),
        'mla_q_norm': gain(ks[19], (N_MLA_LAYERS, Q_LORA)),
        'mla_w_uq': nrm(ks[20], (N_MLA_LAYERS, Q_LORA, MLA_HEADS * (QK_NOPE + QK_ROPE)), Q_LORA ** -0.5),
        'mla_kv_norm': gain(ks[21], (N_MLA_LAYERS, KV_LORA)),
        'mla_w_uk': nrm(ks[22], (N_MLA_LAYERS, KV_LORA, MLA_HEADS, QK_NOPE), KV_LORA ** -0.5),
        'mla_w_uv': nrm(ks[23], (N_MLA_LAYERS, KV_LORA, MLA_HEADS, V_HEAD), KV_LORA ** -0.5),
        'mla_w_out': nrm(ks[24], (N_MLA_LAYERS, MLA_HEADS * V_HEAD + XA_WIDTH, D_MODEL), (MLA_HEADS * V_HEAD + XA_WIDTH) ** -0.5),
        'final_norm': gain(ks[25], (D_MODEL,)),
    }


def reference(x_prompt, x_sample, mem_prompt, mem_sample, ffn1_norm, ffn1_w_in, ffn1_w_out,
              mix_norm, mem_norm, w_mem_kv, ffn2_norm, ffn2_w_in, ffn2_w_out,
              sgu_w_in, sgu_v_norm, sgu_w_s, sgu_b_s, sgu_w_out,
              mla_w_in, mla_q_norm, mla_w_uq, mla_kv_norm, mla_w_uk, mla_w_uv, mla_w_out,
              final_norm):
    p = {
        'ffn1_norm': ffn1_norm, 'ffn1_w_in': ffn1_w_in, 'ffn1_w_out': ffn1_w_out,
        'mix_norm': mix_norm, 'mem_norm': mem_norm, 'w_mem_kv': w_mem_kv,
        'ffn2_norm': ffn2_norm, 'ffn2_w_in': ffn2_w_in, 'ffn2_w_out': ffn2_w_out,
        'sgu_w_in': sgu_w_in, 'sgu_v_norm': sgu_v_norm, 'sgu_w_s': sgu_w_s,
        'sgu_b_s': sgu_b_s, 'sgu_w_out': sgu_w_out,
        'mla_w_in': mla_w_in, 'mla_q_norm': mla_q_norm, 'mla_w_uq': mla_w_uq,
        'mla_kv_norm': mla_kv_norm, 'mla_w_uk': mla_w_uk, 'mla_w_uv': mla_w_uv,
        'mla_w_out': mla_w_out, 'final_norm': final_norm,
    }
    y_prompt = trunk(x_prompt, mem_prompt, p)
    y_sample = trunk(x_sample, mem_sample, p)
    return (y_prompt, y_sample)
```

```python
import functools

import jax
import jax.numpy as jnp
from jax import lax
from jax.experimental import pallas as pl
from jax.experimental.pallas import tpu as pltpu

F32 = jnp.float32
BF16 = jnp.bfloat16

NORM_EPS = 1e-6
ROPE_THETA = 10000.0
XA_HEADS = 4
XA_HEAD_DIM = 128
SGU_GROUPS = 8
SGU_GROUP_DIM = 192
SGU_GROUP_PAD = 256
CHUNK = 128
MLA_HEADS = 8
Q_LORA = 256
KV_LORA = 128
QK_NOPE = 128
QK_ROPE = 64
HALF_ROPE = QK_ROPE // 2
V_HEAD = 128
QK_PAD = 256

VMEM_LIMIT_BYTES = 56 * 1024 * 1024
TOKEN_TILE = 512
ATTN_Q_TILE = 256
ATTN_KV_TILE = TOKEN_TILE


def _params(*semantics):
    return pltpu.CompilerParams(dimension_semantics=semantics,
                                vmem_limit_bytes=VMEM_LIMIT_BYTES)


def _const_spec(shape):
    nd = len(shape)
    return pl.BlockSpec(shape, lambda *_: (0,) * nd, pipeline_mode=pl.Buffered(1))


def _rms(x, g):
    return x * lax.rsqrt(jnp.mean(x * x, axis=-1, keepdims=True) + NORM_EPS) * g


def _gelu_tanh(x):
    return 0.5 * x * (1.0 + jnp.tanh(0.7978845608028654 * (x + 0.044715 * (x * x * x))))


def _dot(a, b):
    return jnp.dot(a, b, preferred_element_type=F32)


def _dot_nt(a, b):
    return lax.dot_general(a, b, (((1,), (1,)), ((), ())), preferred_element_type=F32)


def _ffn_kernel(x_ref, g_ref, win_ref, wout_ref, *rest, d_ff, final):
    if final:
        fg_ref, o_ref = rest
    else:
        (o_ref,) = rest
    x = x_ref[...]
    h = _rms(x, g_ref[...]).astype(BF16)
    gu = _dot(h, win_ref[...])
    gate = gu[:, :d_ff]
    up = gu[:, d_ff:]
    act = (gate * (1.0 / (1.0 + jnp.exp(-gate))) * up).astype(BF16)
    y = x + 0.5 * _dot(act, wout_ref[...])
    if final:
        y = _rms(y, fg_ref[...])
    o_ref[...] = y


def _ffn(x2, g, w_in, w_out, final_g=None):
    n, d = x2.shape
    d_ff = w_out.shape[0]
    tm = TOKEN_TILE
    final = final_g is not None
    in_specs = [pl.BlockSpec((tm, d), lambda i: (i, 0)),
                _const_spec((1, d)), _const_spec(w_in.shape), _const_spec(w_out.shape)]
    args = [x2, g, w_in, w_out]
    if final:
        in_specs.append(_const_spec((1, d)))
        args.append(final_g)
    return pl.pallas_call(
        functools.partial(_ffn_kernel, d_ff=d_ff, final=final),
        out_shape=jax.ShapeDtypeStruct((n, d), F32),
        grid=(n // tm,),
        in_specs=in_specs,
        out_specs=pl.BlockSpec((tm, d), lambda i: (i, 0)),
        compiler_params=_params("arbitrary"),
        name="ffn_final" if final else "ffn",
    )(*args)


def _memkv_kernel(mem_ref, g_ref, w_ref, k_ref, v_ref, *, width):
    h = _rms(mem_ref[0], g_ref[0]).astype(BF16)
    kv = _dot(h, w_ref[0])
    k_ref[0, 0] = kv[:, :width].astype(BF16)
    v_ref[0, 0] = kv[:, width:].astype(BF16)


def _memkv(mem, g, w):
    b, m, d = mem.shape
    nl = w.shape[0]
    width = w.shape[2] // 2
    out = jax.ShapeDtypeStruct((nl, b, m, width), BF16)
    return pl.pallas_call(
        functools.partial(_memkv_kernel, width=width),
        out_shape=(out, out),
        grid=(nl, b),
        in_specs=[pl.BlockSpec((1, m, d), lambda l, i: (i, 0, 0)),
                  pl.BlockSpec((1, 1, d), lambda l, i: (l, 0, 0)),
                  pl.BlockSpec((1, d, 2 * width), lambda l, i: (l, 0, 0))],
        out_specs=(pl.BlockSpec((1, 1, m, width), lambda l, i: (l, i, 0, 0)),
                   pl.BlockSpec((1, 1, m, width), lambda l, i: (l, i, 0, 0))),
        compiler_params=_params("arbitrary", "arbitrary"),
        name="memkv",
    )(mem, g, w)


def _memory_attention(qm, mk_ref, mv_ref):
    outs = []
    for hh in range(XA_HEADS):
        sl = slice(hh * XA_HEAD_DIM, (hh + 1) * XA_HEAD_DIM)
        s = _dot_nt(qm[:, sl].astype(BF16), mk_ref[0, 0, :, sl])
        p = jnp.exp(s - jnp.max(s, axis=-1, keepdims=True))
        l = jnp.sum(p, axis=-1, keepdims=True)
        outs.append(_dot(p.astype(BF16), mv_ref[0, 0, :, sl]) / l)
    return outs


def _sgu_kernel(x_ref, g_ref, wu_ref, wv_ref, wq_ref, vn_ref, ws_ref, bs_ref,
                mk_ref, mv_ref, wog_ref, wox_ref, o_ref, mix_ref, xa_ref, *, tm):
    x = x_ref[0]
    h = _rms(x, g_ref[...]).astype(BF16)
    u = _gelu_tanh(_dot(h, wu_ref[...]))
    v = _gelu_tanh(_dot(h, wv_ref[...]))
    ms = jnp.sum(v * v, axis=-1, keepdims=True) * (1.0 / (SGU_GROUPS * SGU_GROUP_DIM))
    vn = (v * lax.rsqrt(ms + NORM_EPS) * vn_ref[...]).astype(BF16)
    for n in range(tm // CHUNK):
        rows = slice(n * CHUNK, (n + 1) * CHUNK)
        for gi in range(SGU_GROUPS):
            cols = slice(gi * SGU_GROUP_PAD, (gi + 1) * SGU_GROUP_PAD)
            mix_ref[rows, cols] = _dot(ws_ref[gi], vn[rows, cols]) + bs_ref[:, cols]
    gated = (u * mix_ref[...]).astype(BF16)
    qm = _dot(h, wq_ref[...]) * (XA_HEAD_DIM ** -0.5)
    for hh, oh in enumerate(_memory_attention(qm, mk_ref, mv_ref)):
        xa_ref[:, hh * XA_HEAD_DIM:(hh + 1) * XA_HEAD_DIM] = oh.astype(BF16)
    o_ref[0] = x + _dot(gated, wog_ref[...]) + _dot(xa_ref[...], wox_ref[...])


def _sgu(x, w, memk, memv, layer):
    b, s, d = x.shape
    tm = TOKEN_TILE
    gpad = SGU_GROUPS * SGU_GROUP_PAD
    xaw = XA_HEADS * XA_HEAD_DIM
    m = memk.shape[2]
    mem_spec = pl.BlockSpec((1, 1, m, xaw), lambda i, j: (layer, i, 0, 0))
    return pl.pallas_call(
        functools.partial(_sgu_kernel, tm=tm),
        out_shape=jax.ShapeDtypeStruct((b, s, d), F32),
        grid=(b, s // tm),
        in_specs=[pl.BlockSpec((1, tm, d), lambda i, j: (i, j, 0)),
                  _const_spec((1, d)), _const_spec(w["wu"].shape), _const_spec(w["wv"].shape),
                  _const_spec(w["wq"].shape), _const_spec((1, gpad)),
                  _const_spec(w["ws"].shape), _const_spec(w["bs"].shape),
                  mem_spec, mem_spec,
                  _const_spec(w["wog"].shape), _const_spec(w["wox"].shape)],
        out_specs=pl.BlockSpec((1, tm, d), lambda i, j: (i, j, 0)),
        scratch_shapes=[pltpu.VMEM((tm, gpad), F32), pltpu.VMEM((tm, xaw), BF16)],
        compiler_params=_params("arbitrary", "arbitrary"),
        name="sgu_mixer",
    )(x, w["norm"], w["wu"], w["wv"], w["wq"], w["vn"], w["ws"], w["bs"],
      memk, memv, w["wog"], w["wox"])


def _mla_proj_kernel(x_ref, g_ref, wa_ref, qn_ref, kvn_ref, wuq_ref, wuk_ref, cos_ref, sin_ref,
                     mk_ref, mv_ref, q_ref, k_ref, vt_ref, xa_ref):
    x = x_ref[0]
    h = _rms(x, g_ref[...]).astype(BF16)
    proj = _dot(h, wa_ref[...])
    o1 = Q_LORA
    o2 = o1 + KV_LORA
    o3 = o2 + 128
    o4 = o3 + 128
    cos = cos_ref[...]
    sin = sin_ref[...]
    c_q = _rms(proj[:, :o1], qn_ref[...]).astype(BF16)
    c_kv = _rms(proj[:, o1:o2], kvn_ref[...])
    k_rope = proj[:, o2:o3] * cos + proj[:, o3:o4] * sin
    k_ref[0, :, :KV_LORA] = c_kv.astype(BF16)
    k_ref[0, :, KV_LORA:] = k_rope.astype(BF16)
    vt_ref[0, 0] = c_kv.T.astype(BF16)

    q = _dot(c_q, wuq_ref[...])
    hw = MLA_HEADS * QK_NOPE
    scale = (QK_NOPE + QK_ROPE) ** -0.5
    for pair in range(MLA_HEADS // 2):
        sl = slice(pair * 2 * QK_NOPE, (pair + 1) * 2 * QK_NOPE)
        q_lat = _dot(q[:, sl].astype(BF16), wuk_ref[pair])
        for t in range(2):
            hd = pair * 2 + t
            hs = slice(hd * QK_NOPE, (hd + 1) * QK_NOPE)
            q_rope = q[:, hw + hd * 128:hw + (hd + 1) * 128] * cos \
                + q[:, 2 * hw + hd * 128:2 * hw + (hd + 1) * 128] * sin
            q_ref[0, hd, :, :KV_LORA] = (q_lat[:, t * KV_LORA:(t + 1) * KV_LORA] * scale).astype(BF16)
            q_ref[0, hd, :, KV_LORA:] = (q_rope * scale).astype(BF16)
            del hs

    qm = proj[:, o4:] * (XA_HEAD_DIM ** -0.5)
    for hh, oh in enumerate(_memory_attention(qm, mk_ref, mv_ref)):
        xa_ref[0, :, hh * XA_HEAD_DIM:(hh + 1) * XA_HEAD_DIM] = oh.astype(BF16)


def _mla_proj(x, w, cosp, sinp, memk, memv, layer):
    b, s, d = x.shape
    tm = TOKEN_TILE
    xaw = XA_HEADS * XA_HEAD_DIM
    m = memk.shape[2]
    mem_spec = pl.BlockSpec((1, 1, m, xaw), lambda i, j: (layer, i, 0, 0))
    rope_spec = pl.BlockSpec((tm, 128), lambda i, j: (j, 0))
    return pl.pallas_call(
        _mla_proj_kernel,
        out_shape=(jax.ShapeDtypeStruct((b, MLA_HEADS, s, QK_PAD), BF16),
                   jax.ShapeDtypeStruct((b, s, QK_PAD), BF16),
                   jax.ShapeDtypeStruct((b, s // tm, KV_LORA, tm), BF16),
                   jax.ShapeDtypeStruct((b, s, xaw), BF16)),
        grid=(b, s // tm),
        in_specs=[pl.BlockSpec((1, tm, d), lambda i, j: (i, j, 0)),
                  _const_spec((1, d)), _const_spec(w["wa"].shape),
                  _const_spec((1, Q_LORA)), _const_spec((1, KV_LORA)),
                  _const_spec(w["wuq"].shape), _const_spec(w["wuk"].shape),
                  rope_spec, rope_spec, mem_spec, mem_spec],
        out_specs=(pl.BlockSpec((1, MLA_HEADS, tm, QK_PAD), lambda i, j: (i, 0, j, 0)),
                   pl.BlockSpec((1, tm, QK_PAD), lambda i, j: (i, j, 0)),
                   pl.BlockSpec((1, 1, KV_LORA, tm), lambda i, j: (i, j, 0, 0)),
                   pl.BlockSpec((1, tm, xaw), lambda i, j: (i, j, 0))),
        compiler_params=_params("arbitrary", "arbitrary"),
        name="mla_proj",
    )(x, w["norm"], w["wa"], w["qn"], w["kvn"], w["wuq"], w["wuk"], cosp, sinp, memk, memv)


def _attn_kernel(q_ref, k_ref, vt_ref, o_ref, *, tq, tk, n_chunks):
    for hd in range(MLA_HEADS):
        q = q_ref[0, hd]

        def body(c, carry, q=q):
            m, l, acc = carry
            k = k_ref[0, pl.ds(pl.multiple_of(c * tk, tk), tk), :]
            s = _dot_nt(k, q)
            m_new = jnp.maximum(m, jnp.max(s, axis=0, keepdims=True))
            alpha = jnp.exp(m - m_new)
            p = jnp.exp(s - m_new)
            l = alpha * l + jnp.sum(p, axis=0, keepdims=True)
            acc = alpha * acc + _dot(vt_ref[0, c], p.astype(BF16))
            return m_new, l, acc

        init = (jnp.full((1, tq), -jnp.inf, F32), jnp.zeros((1, tq), F32),
                jnp.zeros((KV_LORA, tq), F32))
        _, l, acc = lax.fori_loop(0, n_chunks, body, init)
        o_ref[0, :, hd * KV_LORA:(hd + 1) * KV_LORA] = (acc / l).T.astype(BF16)


def _attention(q, k, vt):
    b, nh, s, dk = q.shape
    tq = ATTN_Q_TILE
    n_chunks, kvl, tk = vt.shape[1:]
    return pl.pallas_call(
        functools.partial(_attn_kernel, tq=tq, tk=tk, n_chunks=n_chunks),
        out_shape=jax.ShapeDtypeStruct((b, s, nh * kvl), BF16),
        grid=(b, s // tq),
        in_specs=[pl.BlockSpec((1, nh, tq, dk), lambda i, j: (i, 0, j, 0)),
                  pl.BlockSpec((1, s, dk), lambda i, j: (i, 0, 0)),
                  pl.BlockSpec((1, n_chunks, kvl, tk), lambda i, j: (i, 0, 0, 0))],
        out_specs=pl.BlockSpec((1, tq, nh * kvl), lambda i, j: (i, j, 0)),
        compiler_params=_params("arbitrary", "arbitrary"),
        name="mla_attention",
    )(q, k, vt)


def _mla_out_kernel(x_ref, ol_ref, xa_ref, wuv_ref, woo_ref, wox_ref, o_ref, o_scr):
    for pair in range(MLA_HEADS // 2):
        sl = slice(pair * 2 * KV_LORA, (pair + 1) * 2 * KV_LORA)
        o_scr[:, sl] = _dot(ol_ref[0, :, sl], wuv_ref[pair]).astype(BF16)
    o_ref[0] = x_ref[0] + _dot(o_scr[...], woo_ref[...]) + _dot(xa_ref[0], wox_ref[...])


def _mla_out(x, o_lat, xa, w):
    b, s, d = x.shape
    tm = TOKEN_TILE
    ow = o_lat.shape[2]
    xaw = xa.shape[2]
    return pl.pallas_call(
        _mla_out_kernel,
        out_shape=jax.ShapeDtypeStruct((b, s, d), F32),
        grid=(b, s // tm),
        in_specs=[pl.BlockSpec((1, tm, d), lambda i, j: (i, j, 0)),
                  pl.BlockSpec((1, tm, ow), lambda i, j: (i, j, 0)),
                  pl.BlockSpec((1, tm, xaw), lambda i, j: (i, j, 0)),
                  _const_spec(w["wuv"].shape), _const_spec(w["woo"].shape),
                  _const_spec(w["wox"].shape)],
        out_specs=pl.BlockSpec((1, tm, d), lambda i, j: (i, j, 0)),
        scratch_shapes=[pltpu.VMEM((tm, MLA_HEADS * V_HEAD), BF16)],
        compiler_params=_params("arbitrary", "arbitrary"),
        name="mla_out",
    )(x, o_lat, xa, w["wuv"], w["woo"], w["wox"])


def _pad_groups(w, axis):
    shp = list(w.shape)
    shp[axis:axis + 1] = [SGU_GROUPS, SGU_GROUP_DIM]
    w = w.reshape(shp)
    pad = [(0, 0)] * len(shp)
    pad[axis + 1] = (0, SGU_GROUP_PAD - SGU_GROUP_DIM)
    w = jnp.pad(w, pad)
    shp[axis:axis + 2] = [SGU_GROUPS * SGU_GROUP_PAD]
    return w.reshape(shp)


def _block_diag_pairs(w):
    n, a, b = w.shape
    z = jnp.zeros((n // 2, a, b), w.dtype)
    top = jnp.concatenate([w[0::2], z], axis=2)
    bot = jnp.concatenate([z, w[1::2]], axis=2)
    return jnp.concatenate([top, bot], axis=1)


def _rope_cols(w1, w2):
    z = jnp.zeros(w1.shape[:-1] + (128 - QK_ROPE,), w1.dtype)
    return jnp.concatenate([w1, w2, z], axis=-1), jnp.concatenate([w2, w1, z], axis=-1)


def _prep_sgu(j, p):
    sw = SGU_GROUPS * SGU_GROUP_DIM
    w_in = p["sgu_w_in"][j]
    w_out = p["sgu_w_out"][j]
    bs = jnp.broadcast_to(p["sgu_b_s"][j].T[:, :, None], (CHUNK, SGU_GROUPS, SGU_GROUP_DIM))
    return {
        "wu": _pad_groups(w_in[:, :sw], 1).astype(BF16),
        "wv": _pad_groups(w_in[:, sw:2 * sw], 1).astype(BF16),
        "wq": w_in[:, 2 * sw:].astype(BF16),
        "vn": _pad_groups(p["sgu_v_norm"][j][None, :], 1),
        "ws": p["sgu_w_s"][j].astype(BF16),
        "bs": _pad_groups(bs.reshape(CHUNK, sw), 1),
        "wog": _pad_groups(w_out[:sw], 0).astype(BF16),
        "wox": w_out[sw:].astype(BF16),
    }


def _prep_mla(j, p):
    w_in = p["mla_w_in"][j]
    o1 = Q_LORA
    o2 = o1 + KV_LORA
    o3 = o2 + QK_ROPE
    kra, krb = _rope_cols(w_in[:, o2:o2 + HALF_ROPE], w_in[:, o2 + HALF_ROPE:o3])
    wa = jnp.concatenate([w_in[:, :o2], kra, krb, w_in[:, o3:]], axis=1)
    wuq = p["mla_w_uq"][j].reshape(Q_LORA, MLA_HEADS, QK_NOPE + QK_ROPE)
    qa, qb = _rope_cols(wuq[:, :, QK_NOPE:QK_NOPE + HALF_ROPE], wuq[:, :, QK_NOPE + HALF_ROPE:])
    wuq_all = jnp.concatenate([wuq[:, :, :QK_NOPE].reshape(Q_LORA, -1),
                               qa.reshape(Q_LORA, -1), qb.reshape(Q_LORA, -1)], axis=1)
    wuk = jnp.transpose(p["mla_w_uk"][j], (1, 2, 0))
    wuv = jnp.transpose(p["mla_w_uv"][j], (1, 0, 2))
    w_out = p["mla_w_out"][j]
    ow = MLA_HEADS * V_HEAD
    return {
        "wa": wa.astype(BF16),
        "qn": p["mla_q_norm"][j][None, :],
        "kvn": p["mla_kv_norm"][j][None, :],
        "wuq": wuq_all.astype(BF16),
        "wuk": _block_diag_pairs(wuk).astype(BF16),
        "wuv": _block_diag_pairs(wuv).astype(BF16),
        "woo": w_out[:ow].astype(BF16),
        "wox": w_out[ow:].astype(BF16),
    }


def _rope_tables(seq_len):
    inv_freq = 1.0 / (ROPE_THETA ** (jnp.arange(0, QK_ROPE, 2, dtype=F32) / QK_ROPE))
    ang = jnp.arange(seq_len, dtype=F32)[:, None] * inv_freq[None, :]
    cos, sin = jnp.cos(ang), jnp.sin(ang)
    z = jnp.zeros((seq_len, 128 - QK_ROPE), F32)
    return (jnp.concatenate([cos, cos, z], axis=1), jnp.concatenate([-sin, sin, z], axis=1))


def _trunk(x, mem, p, prepped):
    b, s, d = x.shape
    depth = p["ffn1_w_in"].shape[0]
    memk, memv = _memkv(mem, p["mem_norm"][:, None, :], prepped["w_mem_kv"])
    cosp, sinp = _rope_tables(s)
    for i in range(depth):
        j = i // 2
        x = _ffn(x.reshape(b * s, d), p["ffn1_norm"][i][None, :],
                 prepped["ffn1_w_in"][i], prepped["ffn1_w_out"][i]).reshape(b, s, d)
        if i % 2 == 0:
            w = dict(prepped["sgu"][j], norm=p["mix_norm"][i][None, :])
            x = _sgu(x, w, memk, memv, i)
        else:
            w = dict(prepped["mla"][j], norm=p["mix_norm"][i][None, :])
            q, k, vt, xa = _mla_proj(x, w, cosp, sinp, memk, memv, i)
            o_lat = _attention(q, k, vt)
            x = _mla_out(x, o_lat, xa, w)
        final_g = p["final_norm"][None, :] if i == depth - 1 else None
        x = _ffn(x.reshape(b * s, d), p["ffn2_norm"][i][None, :],
                 prepped["ffn2_w_in"][i], prepped["ffn2_w_out"][i], final_g).reshape(b, s, d)
    return x


def _prepare(p):
    return {
        "ffn1_w_in": p["ffn1_w_in"].astype(BF16), "ffn1_w_out": p["ffn1_w_out"].astype(BF16),
        "ffn2_w_in": p["ffn2_w_in"].astype(BF16), "ffn2_w_out": p["ffn2_w_out"].astype(BF16),
        "w_mem_kv": p["w_mem_kv"].astype(BF16),
        "sgu": [_prep_sgu(j, p) for j in range(p["sgu_w_in"].shape[0])],
        "mla": [_prep_mla(j, p) for j in range(p["mla_w_in"].shape[0])],
    }


def kernel(x_prompt, x_sample, mem_prompt, mem_sample, ffn1_norm, ffn1_w_in, ffn1_w_out, mix_norm, mem_norm, w_mem_kv, ffn2_norm, ffn2_w_in, ffn2_w_out, sgu_w_in, sgu_v_norm, sgu_w_s, sgu_b_s, sgu_w_out, mla_w_in, mla_q_norm, mla_w_uq, mla_kv_norm, mla_w_uk, mla_w_uv, mla_w_out, final_norm):
    p = {
        'ffn1_norm': ffn1_norm, 'ffn1_w_in': ffn1_w_in, 'ffn1_w_out': ffn1_w_out,
        'mix_norm': mix_norm, 'mem_norm': mem_norm, 'w_mem_kv': w_mem_kv,
        'ffn2_norm': ffn2_norm, 'ffn2_w_in': ffn2_w_in, 'ffn2_w_out': ffn2_w_out,
        'sgu_w_in': sgu_w_in, 'sgu_v_norm': sgu_v_norm, 'sgu_w_s': sgu_w_s,
        'sgu_b_s': sgu_b_s, 'sgu_w_out': sgu_w_out,
        'mla_w_in': mla_w_in, 'mla_q_norm': mla_q_norm, 'mla_w_uq': mla_w_uq,
        'mla_kv_norm': mla_kv_norm, 'mla_w_uk': mla_w_uk, 'mla_w_uv': mla_w_uv,
        'mla_w_out': mla_w_out, 'final_norm': final_norm,
    }
    prepped = _prepare(p)
    return (_trunk(x_prompt, mem_prompt, p, prepped), _trunk(x_sample, mem_sample, p, prepped))
```

```python
import functools

import jax
import jax.numpy as jnp
from jax import lax
from jax.experimental import pallas as pl
from jax.experimental.pallas import tpu as pltpu

F32 = jnp.float32
BF16 = jnp.bfloat16

NORM_EPS = 1e-6
ROPE_THETA = 10000.0
XA_HEADS = 4
XA_HEAD_DIM = 128
SGU_GROUPS = 8
SGU_GROUP_DIM = 192
SGU_GROUP_PAD = 256
CHUNK = 128
MLA_HEADS = 8
Q_LORA = 256
KV_LORA = 128
QK_NOPE = 128
QK_ROPE = 64
HALF_ROPE = QK_ROPE // 2
V_HEAD = 128
QK_PAD = 256
VT_ROWS = KV_LORA + 16
LOG2_E = 1.4426950408889634

VMEM_LIMIT_BYTES = 56 * 1024 * 1024
TOKEN_TILE = 512
ATTN_Q_TILE = 256
ATTN_KV_TILE = TOKEN_TILE


def _params(*semantics):
    return pltpu.CompilerParams(dimension_semantics=semantics,
                                vmem_limit_bytes=VMEM_LIMIT_BYTES)


def _const_spec(shape):
    nd = len(shape)
    return pl.BlockSpec(shape, lambda *_: (0,) * nd, pipeline_mode=pl.Buffered(1))


def _rms(x, g):
    return x * lax.rsqrt(jnp.mean(x * x, axis=-1, keepdims=True) + NORM_EPS) * g


def _gelu_tanh(x):
    return 0.5 * x * (1.0 + jnp.tanh(0.7978845608028654 * (x + 0.044715 * (x * x * x))))


def _dot(a, b):
    return jnp.dot(a, b, preferred_element_type=F32)


def _dot_nt(a, b):
    return lax.dot_general(a, b, (((1,), (1,)), ((), ())), preferred_element_type=F32)


def _ffn_kernel(x_ref, g_ref, win_ref, wout_ref, *rest, d_ff, final):
    if final:
        fg_ref, o_ref = rest
    else:
        (o_ref,) = rest
    x = x_ref[...]
    h = _rms(x, g_ref[...]).astype(BF16)
    gu = _dot(h, win_ref[...])
    gate = gu[:, :d_ff]
    up = gu[:, d_ff:]
    act = (gate * (1.0 / (1.0 + jnp.exp(-gate))) * up).astype(BF16)
    y = x + 0.5 * _dot(act, wout_ref[...])
    if final:
        y = _rms(y, fg_ref[...])
    o_ref[...] = y


def _ffn(x2, g, w_in, w_out, final_g=None):
    n, d = x2.shape
    d_ff = w_out.shape[0]
    tm = TOKEN_TILE
    final = final_g is not None
    in_specs = [pl.BlockSpec((tm, d), lambda i: (i, 0)),
                _const_spec((1, d)), _const_spec(w_in.shape), _const_spec(w_out.shape)]
    args = [x2, g, w_in, w_out]
    if final:
        in_specs.append(_const_spec((1, d)))
        args.append(final_g)
    return pl.pallas_call(
        functools.partial(_ffn_kernel, d_ff=d_ff, final=final),
        out_shape=jax.ShapeDtypeStruct((n, d), F32),
        grid=(n // tm,),
        in_specs=in_specs,
        out_specs=pl.BlockSpec((tm, d), lambda i: (i, 0)),
        compiler_params=_params("arbitrary"),
        name="ffn_final" if final else "ffn",
    )(*args)


def _memkv_kernel(mem_ref, g_ref, w_ref, k_ref, v_ref, *, width):
    h = _rms(mem_ref[0], g_ref[0]).astype(BF16)
    kv = _dot(h, w_ref[0])
    k_ref[0, 0] = kv[:, :width].astype(BF16)
    v_ref[0, 0] = kv[:, width:].astype(BF16)


def _memkv(mem, g, w):
    b, m, d = mem.shape
    nl = w.shape[0]
    width = w.shape[2] // 2
    out = jax.ShapeDtypeStruct((nl, b, m, width), BF16)
    return pl.pallas_call(
        functools.partial(_memkv_kernel, width=width),
        out_shape=(out, out),
        grid=(nl, b),
        in_specs=[pl.BlockSpec((1, m, d), lambda l, i: (i, 0, 0)),
                  pl.BlockSpec((1, 1, d), lambda l, i: (l, 0, 0)),
                  pl.BlockSpec((1, d, 2 * width), lambda l, i: (l, 0, 0))],
        out_specs=(pl.BlockSpec((1, 1, m, width), lambda l, i: (l, i, 0, 0)),
                   pl.BlockSpec((1, 1, m, width), lambda l, i: (l, i, 0, 0))),
        compiler_params=_params("arbitrary", "arbitrary"),
        name="memkv",
    )(mem, g, w)


def _memory_attention(qm, mk_ref, mv_ref):
    outs = []
    for hh in range(XA_HEADS):
        sl = slice(hh * XA_HEAD_DIM, (hh + 1) * XA_HEAD_DIM)
        s = _dot_nt(qm[:, sl].astype(BF16), mk_ref[0, 0, :, sl])
        p = jnp.exp(s - jnp.max(s, axis=-1, keepdims=True))
        l = jnp.sum(p, axis=-1, keepdims=True)
        outs.append(_dot(p.astype(BF16), mv_ref[0, 0, :, sl]) / l)
    return outs


def _sgu_kernel(x_ref, g_ref, wu_ref, wv_ref, wq_ref, vn_ref, ws_ref, bs_ref,
                mk_ref, mv_ref, wog_ref, wox_ref, o_ref, mix_ref, xa_ref, *, tm):
    x = x_ref[0]
    h = _rms(x, g_ref[...]).astype(BF16)
    u = _gelu_tanh(_dot(h, wu_ref[...]))
    v = _gelu_tanh(_dot(h, wv_ref[...]))
    ms = jnp.sum(v * v, axis=-1, keepdims=True) * (1.0 / (SGU_GROUPS * SGU_GROUP_DIM))
    vn = (v * lax.rsqrt(ms + NORM_EPS) * vn_ref[...]).astype(BF16)
    for n in range(tm // CHUNK):
        rows = slice(n * CHUNK, (n + 1) * CHUNK)
        for gi in range(SGU_GROUPS):
            cols = slice(gi * SGU_GROUP_PAD, (gi + 1) * SGU_GROUP_PAD)
            mix_ref[rows, cols] = _dot(ws_ref[gi], vn[rows, cols]) + bs_ref[:, cols]
    gated = (u * mix_ref[...]).astype(BF16)
    qm = _dot(h, wq_ref[...]) * (XA_HEAD_DIM ** -0.5)
    for hh, oh in enumerate(_memory_attention(qm, mk_ref, mv_ref)):
        xa_ref[:, hh * XA_HEAD_DIM:(hh + 1) * XA_HEAD_DIM] = oh.astype(BF16)
    o_ref[0] = x + _dot(gated, wog_ref[...]) + _dot(xa_ref[...], wox_ref[...])


def _sgu(x, w, memk, memv, layer):
    b, s, d = x.shape
    tm = TOKEN_TILE
    gpad = SGU_GROUPS * SGU_GROUP_PAD
    xaw = XA_HEADS * XA_HEAD_DIM
    m = memk.shape[2]
    mem_spec = pl.BlockSpec((1, 1, m, xaw), lambda i, j: (layer, i, 0, 0))
    return pl.pallas_call(
        functools.partial(_sgu_kernel, tm=tm),
        out_shape=jax.ShapeDtypeStruct((b, s, d), F32),
        grid=(b, s // tm),
        in_specs=[pl.BlockSpec((1, tm, d), lambda i, j: (i, j, 0)),
                  _const_spec((1, d)), _const_spec(w["wu"].shape), _const_spec(w["wv"].shape),
                  _const_spec(w["wq"].shape), _const_spec((1, gpad)),
                  _const_spec(w["ws"].shape), _const_spec(w["bs"].shape),
                  mem_spec, mem_spec,
                  _const_spec(w["wog"].shape), _const_spec(w["wox"].shape)],
        out_specs=pl.BlockSpec((1, tm, d), lambda i, j: (i, j, 0)),
        scratch_shapes=[pltpu.VMEM((tm, gpad), F32), pltpu.VMEM((tm, xaw), BF16)],
        compiler_params=_params("arbitrary", "arbitrary"),
        name="sgu_mixer",
    )(x, w["norm"], w["wu"], w["wv"], w["wq"], w["vn"], w["ws"], w["bs"],
      memk, memv, w["wog"], w["wox"])


def _mla_proj_kernel(x_ref, g_ref, wa_ref, qn_ref, kvn_ref, wuq_ref, wuk_ref, cos_ref, sin_ref,
                     mk_ref, mv_ref, q_ref, k_ref, vt_ref, xa_ref):
    x = x_ref[0]
    h = _rms(x, g_ref[...]).astype(BF16)
    proj = _dot(h, wa_ref[...])
    o1 = Q_LORA
    o2 = o1 + KV_LORA
    o3 = o2 + 128
    o4 = o3 + 128
    cos = cos_ref[...]
    sin = sin_ref[...]
    c_q = _rms(proj[:, :o1], qn_ref[...]).astype(BF16)
    c_kv = _rms(proj[:, o1:o2], kvn_ref[...])
    k_rope = proj[:, o2:o3] * cos + proj[:, o3:o4] * sin
    k_ref[0, :, :KV_LORA] = c_kv.astype(BF16)
    k_ref[0, :, KV_LORA:] = k_rope.astype(BF16)
    vt_ref[0, 0, :KV_LORA, :] = c_kv.T.astype(BF16)
    vt_ref[0, 0, KV_LORA:, :] = jnp.ones((VT_ROWS - KV_LORA, c_kv.shape[0]), BF16)

    q = _dot(c_q, wuq_ref[...])
    hw = MLA_HEADS * QK_NOPE
    scale = (QK_NOPE + QK_ROPE) ** -0.5 * LOG2_E
    for pair in range(MLA_HEADS // 2):
        sl = slice(pair * 2 * QK_NOPE, (pair + 1) * 2 * QK_NOPE)
        q_lat = _dot(q[:, sl].astype(BF16), wuk_ref[pair])
        for t in range(2):
            hd = pair * 2 + t
            q_rope = q[:, hw + hd * 128:hw + (hd + 1) * 128] * cos \
                + q[:, 2 * hw + hd * 128:2 * hw + (hd + 1) * 128] * sin
            q_ref[0, hd, :, :KV_LORA] = (q_lat[:, t * KV_LORA:(t + 1) * KV_LORA] * scale).astype(BF16)
            q_ref[0, hd, :, KV_LORA:] = (q_rope * scale).astype(BF16)

    qm = proj[:, o4:] * (XA_HEAD_DIM ** -0.5)
    for hh, oh in enumerate(_memory_attention(qm, mk_ref, mv_ref)):
        xa_ref[0, :, hh * XA_HEAD_DIM:(hh + 1) * XA_HEAD_DIM] = oh.astype(BF16)


def _mla_proj(x, w, cosp, sinp, memk, memv, layer):
    b, s, d = x.shape
    tm = TOKEN_TILE
    xaw = XA_HEADS * XA_HEAD_DIM
    m = memk.shape[2]
    mem_spec = pl.BlockSpec((1, 1, m, xaw), lambda i, j: (layer, i, 0, 0))
    rope_spec = pl.BlockSpec((tm, 128), lambda i, j: (j, 0))
    return pl.pallas_call(
        _mla_proj_kernel,
        out_shape=(jax.ShapeDtypeStruct((b, MLA_HEADS, s, QK_PAD), BF16),
                   jax.ShapeDtypeStruct((b, s, QK_PAD), BF16),
                   jax.ShapeDtypeStruct((b, s // tm, VT_ROWS, tm), BF16),
                   jax.ShapeDtypeStruct((b, s, xaw), BF16)),
        grid=(b, s // tm),
        in_specs=[pl.BlockSpec((1, tm, d), lambda i, j: (i, j, 0)),
                  _const_spec((1, d)), _const_spec(w["wa"].shape),
                  _const_spec((1, Q_LORA)), _const_spec((1, KV_LORA)),
                  _const_spec(w["wuq"].shape), _const_spec(w["wuk"].shape),
                  rope_spec, rope_spec, mem_spec, mem_spec],
        out_specs=(pl.BlockSpec((1, MLA_HEADS, tm, QK_PAD), lambda i, j: (i, 0, j, 0)),
                   pl.BlockSpec((1, tm, QK_PAD), lambda i, j: (i, j, 0)),
                   pl.BlockSpec((1, 1, VT_ROWS, tm), lambda i, j: (i, j, 0, 0)),
                   pl.BlockSpec((1, tm, xaw), lambda i, j: (i, j, 0))),
        compiler_params=_params("arbitrary", "arbitrary"),
        name="mla_proj",
    )(x, w["norm"], w["wa"], w["qn"], w["kvn"], w["wuq"], w["wuk"], cosp, sinp, memk, memv)


def _attn_kernel(q_ref, k_ref, vt_ref, o_ref, acc_ref, m_ref, s_ref, cm_ref, *, tk, n_chunks):
    heads = range(MLA_HEADS)
    acc_ref[...] = jnp.zeros(acc_ref.shape, F32)
    m_ref[...] = jnp.full(m_ref.shape, -jnp.inf, F32)

    def scores(c, slot):
        k = k_ref[0, pl.ds(pl.multiple_of(c * tk, tk), tk), :]
        for hd in heads:
            s = _dot_nt(k, q_ref[0, hd])
            s_ref[slot, hd] = s
            cm_ref[slot, hd] = jnp.max(s, axis=0, keepdims=True)

    def update(c, slot):
        vt = vt_ref[0, c]
        m_old = [m_ref[hd] for hd in heads]
        m_new = [jnp.maximum(m_old[hd], cm_ref[slot, hd]) for hd in heads]
        ps = [jnp.exp2(s_ref[slot, hd] - m_new[hd]).astype(BF16) for hd in heads]
        pvs = [_dot(vt, ps[hd]) for hd in heads]
        for hd in heads:
            acc_ref[hd] = jnp.exp2(m_old[hd] - m_new[hd]) * acc_ref[hd] + pvs[hd]
            m_ref[hd] = m_new[hd]

    def pair(i, carry):
        c = 2 * i
        scores(c + 1, 1)
        update(c, 0)
        scores(c + 2, 0)
        update(c + 1, 1)
        return carry

    scores(0, 0)
    lax.fori_loop(0, n_chunks // 2 - 1, pair, 0)
    scores(n_chunks - 1, 1)
    update(n_chunks - 2, 0)
    update(n_chunks - 1, 1)
    for hd in heads:
        acc = acc_ref[hd]
        o = acc[:KV_LORA] / acc[KV_LORA:KV_LORA + 1]
        o_ref[0, :, hd * KV_LORA:(hd + 1) * KV_LORA] = o.T.astype(BF16)


def _attention(q, k, vt):
    b, nh, s, dk = q.shape
    tq = ATTN_Q_TILE
    n_chunks, vt_rows, tk = vt.shape[1:]
    assert n_chunks % 2 == 0 and n_chunks >= 2, n_chunks
    return pl.pallas_call(
        functools.partial(_attn_kernel, tk=tk, n_chunks=n_chunks),
        out_shape=jax.ShapeDtypeStruct((b, s, nh * KV_LORA), BF16),
        grid=(b, s // tq),
        in_specs=[pl.BlockSpec((1, nh, tq, dk), lambda i, j: (i, 0, j, 0)),
                  pl.BlockSpec((1, s, dk), lambda i, j: (i, 0, 0), pipeline_mode=pl.Buffered(1)),
                  pl.BlockSpec((1, n_chunks, vt_rows, tk), lambda i, j: (i, 0, 0, 0),
                               pipeline_mode=pl.Buffered(1))],
        out_specs=pl.BlockSpec((1, tq, nh * KV_LORA), lambda i, j: (i, j, 0)),
        scratch_shapes=[pltpu.VMEM((nh, vt_rows, tq), F32), pltpu.VMEM((nh, 1, tq), F32),
                        pltpu.VMEM((2, nh, tk, tq), F32), pltpu.VMEM((2, nh, 1, tq), F32)],
        compiler_params=_params("arbitrary", "arbitrary"),
        name="mla_attention",
    )(q, k, vt)


def _mla_out_kernel(x_ref, ol_ref, xa_ref, wuv_ref, woo_ref, wox_ref, o_ref, o_scr):
    for pair in range(MLA_HEADS // 2):
        sl = slice(pair * 2 * KV_LORA, (pair + 1) * 2 * KV_LORA)
        o_scr[:, sl] = _dot(ol_ref[0, :, sl], wuv_ref[pair]).astype(BF16)
    o_ref[0] = x_ref[0] + _dot(o_scr[...], woo_ref[...]) + _dot(xa_ref[0], wox_ref[...])


def _mla_out(x, o_lat, xa, w):
    b, s, d = x.shape
    tm = TOKEN_TILE
    ow = o_lat.shape[2]
    xaw = xa.shape[2]
    return pl.pallas_call(
        _mla_out_kernel,
        out_shape=jax.ShapeDtypeStruct((b, s, d), F32),
        grid=(b, s // tm),
        in_specs=[pl.BlockSpec((1, tm, d), lambda i, j: (i, j, 0)),
                  pl.BlockSpec((1, tm, ow), lambda i, j: (i, j, 0)),
                  pl.BlockSpec((1, tm, xaw), lambda i, j: (i, j, 0)),
                  _const_spec(w["wuv"].shape), _const_spec(w["woo"].shape),
                  _const_spec(w["wox"].shape)],
        out_specs=pl.BlockSpec((1, tm, d), lambda i, j: (i, j, 0)),
        scratch_shapes=[pltpu.VMEM((tm, MLA_HEADS * V_HEAD), BF16)],
        compiler_params=_params("arbitrary", "arbitrary"),
        name="mla_out",
    )(x, o_lat, xa, w["wuv"], w["woo"], w["wox"])


def _pad_groups(w, axis):
    shp = list(w.shape)
    shp[axis:axis + 1] = [SGU_GROUPS, SGU_GROUP_DIM]
    w = w.reshape(shp)
    pad = [(0, 0)] * len(shp)
    pad[axis + 1] = (0, SGU_GROUP_PAD - SGU_GROUP_DIM)
    w = jnp.pad(w, pad)
    shp[axis:axis + 2] = [SGU_GROUPS * SGU_GROUP_PAD]
    return w.reshape(shp)


def _block_diag_pairs(w):
    n, a, b = w.shape
    z = jnp.zeros((n // 2, a, b), w.dtype)
    top = jnp.concatenate([w[0::2], z], axis=2)
    bot = jnp.concatenate([z, w[1::2]], axis=2)
    return jnp.concatenate([top, bot], axis=1)


def _rope_cols(w1, w2):
    z = jnp.zeros(w1.shape[:-1] + (128 - QK_ROPE,), w1.dtype)
    return jnp.concatenate([w1, w2, z], axis=-1), jnp.concatenate([w2, w1, z], axis=-1)


def _prep_sgu(j, p):
    sw = SGU_GROUPS * SGU_GROUP_DIM
    w_in = p["sgu_w_in"][j]
    w_out = p["sgu_w_out"][j]
    bs = jnp.broadcast_to(p["sgu_b_s"][j].T[:, :, None], (CHUNK, SGU_GROUPS, SGU_GROUP_DIM))
    return {
        "wu": _pad_groups(w_in[:, :sw], 1).astype(BF16),
        "wv": _pad_groups(w_in[:, sw:2 * sw], 1).astype(BF16),
        "wq": w_in[:, 2 * sw:].astype(BF16),
        "vn": _pad_groups(p["sgu_v_norm"][j][None, :], 1),
        "ws": p["sgu_w_s"][j].astype(BF16),
        "bs": _pad_groups(bs.reshape(CHUNK, sw), 1),
        "wog": _pad_groups(w_out[:sw], 0).astype(BF16),
        "wox": w_out[sw:].astype(BF16),
    }


def _prep_mla(j, p):
    w_in = p["mla_w_in"][j]
    o1 = Q_LORA
    o2 = o1 + KV_LORA
    o3 = o2 + QK_ROPE
    kra, krb = _rope_cols(w_in[:, o2:o2 + HALF_ROPE], w_in[:, o2 + HALF_ROPE:o3])
    wa = jnp.concatenate([w_in[:, :o2], kra, krb, w_in[:, o3:]], axis=1)
    wuq = p["mla_w_uq"][j].reshape(Q_LORA, MLA_HEADS, QK_NOPE + QK_ROPE)
    qa, qb = _rope_cols(wuq[:, :, QK_NOPE:QK_NOPE + HALF_ROPE], wuq[:, :, QK_NOPE + HALF_ROPE:])
    wuq_all = jnp.concatenate([wuq[:, :, :QK_NOPE].reshape(Q_LORA, -1),
                               qa.reshape(Q_LORA, -1), qb.reshape(Q_LORA, -1)], axis=1)
    wuk = jnp.transpose(p["mla_w_uk"][j], (1, 2, 0))
    wuv = jnp.transpose(p["mla_w_uv"][j], (1, 0, 2))
    w_out = p["mla_w_out"][j]
    ow = MLA_HEADS * V_HEAD
    return {
        "wa": wa.astype(BF16),
        "qn": p["mla_q_norm"][j][None, :],
        "kvn": p["mla_kv_norm"][j][None, :],
        "wuq": wuq_all.astype(BF16),
        "wuk": _block_diag_pairs(wuk).astype(BF16),
        "wuv": _block_diag_pairs(wuv).astype(BF16),
        "woo": w_out[:ow].astype(BF16),
        "wox": w_out[ow:].astype(BF16),
    }


def _rope_tables(seq_len):
    inv_freq = 1.0 / (ROPE_THETA ** (jnp.arange(0, QK_ROPE, 2, dtype=F32) / QK_ROPE))
    ang = jnp.arange(seq_len, dtype=F32)[:, None] * inv_freq[None, :]
    cos, sin = jnp.cos(ang), jnp.sin(ang)
    z = jnp.zeros((seq_len, 128 - QK_ROPE), F32)
    return (jnp.concatenate([cos, cos, z], axis=1), jnp.concatenate([-sin, sin, z], axis=1))


def _trunk(x, mem, p, prepped):
    b, s, d = x.shape
    depth = p["ffn1_w_in"].shape[0]
    memk, memv = _memkv(mem, p["mem_norm"][:, None, :], prepped["w_mem_kv"])
    cosp, sinp = _rope_tables(s)
    for i in range(depth):
        j = i // 2
        x = _ffn(x.reshape(b * s, d), p["ffn1_norm"][i][None, :],
                 prepped["ffn1_w_in"][i], prepped["ffn1_w_out"][i]).reshape(b, s, d)
        if i % 2 == 0:
            w = dict(prepped["sgu"][j], norm=p["mix_norm"][i][None, :])
            x = _sgu(x, w, memk, memv, i)
        else:
            w = dict(prepped["mla"][j], norm=p["mix_norm"][i][None, :])
            q, k, vt, xa = _mla_proj(x, w, cosp, sinp, memk, memv, i)
            o_lat = _attention(q, k, vt)
            x = _mla_out(x, o_lat, xa, w)
        final_g = p["final_norm"][None, :] if i == depth - 1 else None
        x = _ffn(x.reshape(b * s, d), p["ffn2_norm"][i][None, :],
                 prepped["ffn2_w_in"][i], prepped["ffn2_w_out"][i], final_g).reshape(b, s, d)
    return x


def _prepare(p):
    return {
        "ffn1_w_in": p["ffn1_w_in"].astype(BF16), "ffn1_w_out": p["ffn1_w_out"].astype(BF16),
        "ffn2_w_in": p["ffn2_w_in"].astype(BF16), "ffn2_w_out": p["ffn2_w_out"].astype(BF16),
        "w_mem_kv": p["w_mem_kv"].astype(BF16),
        "sgu": [_prep_sgu(j, p) for j in range(p["sgu_w_in"].shape[0])],
        "mla": [_prep_mla(j, p) for j in range(p["mla_w_in"].shape[0])],
    }


def kernel(x_prompt, x_sample, mem_prompt, mem_sample, ffn1_norm, ffn1_w_in, ffn1_w_out, mix_norm, mem_norm, w_mem_kv, ffn2_norm, ffn2_w_in, ffn2_w_out, sgu_w_in, sgu_v_norm, sgu_w_s, sgu_b_s, sgu_w_out, mla_w_in, mla_q_norm, mla_w_uq, mla_kv_norm, mla_w_uk, mla_w_uv, mla_w_out, final_norm):
    p = {
        'ffn1_norm': ffn1_norm, 'ffn1_w_in': ffn1_w_in, 'ffn1_w_out': ffn1_w_out,
        'mix_norm': mix_norm, 'mem_norm': mem_norm, 'w_mem_kv': w_mem_kv,
        'ffn2_norm': ffn2_norm, 'ffn2_w_in': ffn2_w_in, 'ffn2_w_out': ffn2_w_out,
        'sgu_w_in': sgu_w_in, 'sgu_v_norm': sgu_v_norm, 'sgu_w_s': sgu_w_s,
        'sgu_b_s': sgu_b_s, 'sgu_w_out': sgu_w_out,
        'mla_w_in': mla_w_in, 'mla_q_norm': mla_q_norm, 'mla_w_uq': mla_w_uq,
        'mla_kv_norm': mla_kv_norm, 'mla_w_uk': mla_w_uk, 'mla_w_uv': mla_w_uv,
        'mla_w_out': mla_w_out, 'final_norm': final_norm,
    }
    prepped = _prepare(p)
    return (_trunk(x_prompt, mem_prompt, p, prepped), _trunk(x_sample, mem_sample, p, prepped))
```

```python
import functools

import jax
import jax.numpy as jnp
from jax import lax
from jax.experimental import pallas as pl
from jax.experimental.pallas import tpu as pltpu

F32 = jnp.float32
BF16 = jnp.bfloat16

NORM_EPS = 1e-6
ROPE_THETA = 10000.0
XA_HEADS = 4
XA_HEAD_DIM = 128
SGU_GROUPS = 8
SGU_GROUP_DIM = 192
CHUNK = 128
MLA_HEADS = 8
Q_LORA = 256
KV_LORA = 128
QK_NOPE = 128
QK_ROPE = 64
HALF_ROPE = QK_ROPE // 2
V_HEAD = 128
QK_PAD = 256
VT_ROWS = KV_LORA + 16
LOG2_E = 1.4426950408889634

VMEM_LIMIT_BYTES = 56 * 1024 * 1024
TOKEN_TILE = 512
ATTN_Q_TILE = 256
ATTN_UNROLL = 4


def _params(*semantics):
    return pltpu.CompilerParams(dimension_semantics=semantics,
                                vmem_limit_bytes=VMEM_LIMIT_BYTES)


def _const_spec(shape):
    nd = len(shape)
    return pl.BlockSpec(shape, lambda *_: (0,) * nd, pipeline_mode=pl.Buffered(1))


def _rms(x, g):
    return x * lax.rsqrt(jnp.mean(x * x, axis=-1, keepdims=True) + NORM_EPS) * g


def _gelu_tanh(x):
    return 0.5 * x * (1.0 + jnp.tanh(0.7978845608028654 * (x + 0.044715 * (x * x * x))))


def _dot(a, b):
    return jnp.dot(a, b, preferred_element_type=F32)


def _dot_nt(a, b):
    return lax.dot_general(a, b, (((1,), (1,)), ((), ())), preferred_element_type=F32)


def _ffn_kernel(x_ref, g_ref, win_ref, wout_ref, *rest, d_ff, final):
    if final:
        fg_ref, o_ref = rest
    else:
        (o_ref,) = rest
    x = x_ref[...]
    h = _rms(x, g_ref[...]).astype(BF16)
    gu = _dot(h, win_ref[...])
    gate = gu[:, :d_ff]
    up = gu[:, d_ff:]
    act = (gate * (1.0 / (1.0 + jnp.exp(-gate))) * up).astype(BF16)
    y = x + 0.5 * _dot(act, wout_ref[...])
    if final:
        y = _rms(y, fg_ref[...])
    o_ref[...] = y


def _ffn(x2, g, w_in, w_out, final_g=None):
    n, d = x2.shape
    d_ff = w_out.shape[0]
    tm = TOKEN_TILE
    final = final_g is not None
    in_specs = [pl.BlockSpec((tm, d), lambda i: (i, 0)),
                _const_spec((1, d)), _const_spec(w_in.shape), _const_spec(w_out.shape)]
    args = [x2, g, w_in, w_out]
    if final:
        in_specs.append(_const_spec((1, d)))
        args.append(final_g)
    return pl.pallas_call(
        functools.partial(_ffn_kernel, d_ff=d_ff, final=final),
        out_shape=jax.ShapeDtypeStruct((n, d), F32),
        grid=(n // tm,),
        in_specs=in_specs,
        out_specs=pl.BlockSpec((tm, d), lambda i: (i, 0)),
        compiler_params=_params("arbitrary"),
        name="ffn_final" if final else "ffn",
    )(*args)


def _memkv_kernel(mem_ref, g_ref, w_ref, k_ref, v_ref, *, width):
    h = _rms(mem_ref[0], g_ref[0]).astype(BF16)
    kv = _dot(h, w_ref[0])
    k_ref[0, 0] = kv[:, :width].astype(BF16)
    v_ref[0, 0] = kv[:, width:].astype(BF16)


def _memkv(mem, g, w):
    b, m, d = mem.shape
    nl = w.shape[0]
    width = w.shape[2] // 2
    out = jax.ShapeDtypeStruct((nl, b, m, width), BF16)
    return pl.pallas_call(
        functools.partial(_memkv_kernel, width=width),
        out_shape=(out, out),
        grid=(nl, b),
        in_specs=[pl.BlockSpec((1, m, d), lambda l, i: (i, 0, 0)),
                  pl.BlockSpec((1, 1, d), lambda l, i: (l, 0, 0)),
                  pl.BlockSpec((1, d, 2 * width), lambda l, i: (l, 0, 0))],
        out_specs=(pl.BlockSpec((1, 1, m, width), lambda l, i: (l, i, 0, 0)),
                   pl.BlockSpec((1, 1, m, width), lambda l, i: (l, i, 0, 0))),
        compiler_params=_params("arbitrary", "arbitrary"),
        name="memkv",
    )(mem, g, w)


def _memory_attention(qm, mk_ref, mv_ref):
    outs = []
    for hh in range(XA_HEADS):
        sl = slice(hh * XA_HEAD_DIM, (hh + 1) * XA_HEAD_DIM)
        s = _dot_nt(qm[:, sl].astype(BF16), mk_ref[0, 0, :, sl])
        p = jnp.exp(s - jnp.max(s, axis=-1, keepdims=True))
        l = jnp.sum(p, axis=-1, keepdims=True)
        outs.append(_dot(p.astype(BF16), mv_ref[0, 0, :, sl]) / l)
    return outs


def _sgu_kernel(x_ref, g_ref, win_ref, vn_ref, ws_ref, bs_ref, mk_ref, mv_ref, wout_ref,
                o_ref, cat_ref, *, tm):
    sw = SGU_GROUPS * SGU_GROUP_DIM
    pair_w = 2 * SGU_GROUP_DIM
    x = x_ref[0]
    h = _rms(x, g_ref[...]).astype(BF16)
    proj = _dot(h, win_ref[...])
    v = _gelu_tanh(proj[:, sw:2 * sw])
    ms = jnp.mean(v * v, axis=-1, keepdims=True)
    vn = (v * lax.rsqrt(ms + NORM_EPS) * vn_ref[...]).astype(BF16)
    first_half = lax.broadcasted_iota(jnp.int32, (CHUNK, 128), 1) < (SGU_GROUP_DIM - 128)
    for n in range(tm // CHUNK):
        rows = slice(n * CHUNK, (n + 1) * CHUNK)
        for j in range(SGU_GROUPS // 2):
            c0 = j * pair_w
            r0 = _dot(ws_ref[2 * j], vn[rows, c0:c0 + 256])
            r1 = _dot(ws_ref[2 * j + 1], vn[rows, c0 + 128:c0 + 384])
            mixed = (r0[:, :128], jnp.where(first_half, r0[:, 128:], r1[:, :128]), r1[:, 128:])
            for t in range(3):
                cols = slice(c0 + t * 128, c0 + (t + 1) * 128)
                u = _gelu_tanh(proj[rows, cols])
                cat_ref[rows, cols] = (u * (mixed[t] + bs_ref[:, cols])).astype(BF16)
    qm = proj[:, 2 * sw:] * (XA_HEAD_DIM ** -0.5)
    for hh, oh in enumerate(_memory_attention(qm, mk_ref, mv_ref)):
        cat_ref[:, sw + hh * XA_HEAD_DIM:sw + (hh + 1) * XA_HEAD_DIM] = oh.astype(BF16)
    o_ref[0] = x + _dot(cat_ref[...], wout_ref[...])


def _sgu(x, w, memk, memv, layer):
    b, s, d = x.shape
    tm = TOKEN_TILE
    sw = SGU_GROUPS * SGU_GROUP_DIM
    xaw = XA_HEADS * XA_HEAD_DIM
    m = memk.shape[2]
    mem_spec = pl.BlockSpec((1, 1, m, xaw), lambda i, j: (layer, i, 0, 0))
    return pl.pallas_call(
        functools.partial(_sgu_kernel, tm=tm),
        out_shape=jax.ShapeDtypeStruct((b, s, d), F32),
        grid=(b, s // tm),
        in_specs=[pl.BlockSpec((1, tm, d), lambda i, j: (i, j, 0)),
                  _const_spec((1, d)), _const_spec(w["win"].shape), _const_spec((1, sw)),
                  _const_spec(w["ws"].shape), _const_spec(w["bs"].shape),
                  mem_spec, mem_spec, _const_spec(w["wout"].shape)],
        out_specs=pl.BlockSpec((1, tm, d), lambda i, j: (i, j, 0)),
        scratch_shapes=[pltpu.VMEM((tm, sw + xaw), BF16)],
        compiler_params=_params("arbitrary", "arbitrary"),
        name="sgu_mixer",
    )(x, w["norm"], w["win"], w["vn"], w["ws"], w["bs"], memk, memv, w["wout"])


def _mla_proj_kernel(x_ref, g_ref, wa_ref, qn_ref, kvn_ref, wuq_ref, wuk_ref, cos_ref, sin_ref,
                     mk_ref, mv_ref, q_ref, k_ref, vt_ref, xa_ref):
    x = x_ref[0]
    h = _rms(x, g_ref[...]).astype(BF16)
    proj = _dot(h, wa_ref[...])
    o1 = Q_LORA
    o2 = o1 + KV_LORA
    o3 = o2 + 128
    o4 = o3 + 128
    cos = cos_ref[...]
    sin = sin_ref[...]
    c_q = _rms(proj[:, :o1], qn_ref[...]).astype(BF16)
    c_kv = _rms(proj[:, o1:o2], kvn_ref[...])
    k_rope = proj[:, o2:o3] * cos + proj[:, o3:o4] * sin
    k_ref[0, :, :KV_LORA] = c_kv.astype(BF16)
    k_ref[0, :, KV_LORA:] = k_rope.astype(BF16)
    vt_ref[0, 0, :KV_LORA, :] = c_kv.T.astype(BF16)
    vt_ref[0, 0, KV_LORA:, :] = jnp.ones((VT_ROWS - KV_LORA, c_kv.shape[0]), BF16)

    q = _dot(c_q, wuq_ref[...])
    hw = MLA_HEADS * QK_NOPE
    scale = (QK_NOPE + QK_ROPE) ** -0.5 * LOG2_E
    for pair in range(MLA_HEADS // 2):
        sl = slice(pair * 2 * QK_NOPE, (pair + 1) * 2 * QK_NOPE)
        q_lat = _dot(q[:, sl].astype(BF16), wuk_ref[pair])
        for t in range(2):
            hd = pair * 2 + t
            q_rope = q[:, hw + hd * 128:hw + (hd + 1) * 128] * cos \
                + q[:, 2 * hw + hd * 128:2 * hw + (hd + 1) * 128] * sin
            q_ref[0, hd, :, :KV_LORA] = (q_lat[:, t * KV_LORA:(t + 1) * KV_LORA] * scale).astype(BF16)
            q_ref[0, hd, :, KV_LORA:] = (q_rope * scale).astype(BF16)

    qm = proj[:, o4:] * (XA_HEAD_DIM ** -0.5)
    for hh, oh in enumerate(_memory_attention(qm, mk_ref, mv_ref)):
        xa_ref[0, :, hh * XA_HEAD_DIM:(hh + 1) * XA_HEAD_DIM] = oh.astype(BF16)


def _mla_proj(x, w, cosp, sinp, memk, memv, layer):
    b, s, d = x.shape
    tm = TOKEN_TILE
    xaw = XA_HEADS * XA_HEAD_DIM
    m = memk.shape[2]
    mem_spec = pl.BlockSpec((1, 1, m, xaw), lambda i, j: (layer, i, 0, 0))
    rope_spec = pl.BlockSpec((tm, 128), lambda i, j: (j, 0))
    return pl.pallas_call(
        _mla_proj_kernel,
        out_shape=(jax.ShapeDtypeStruct((b, MLA_HEADS, s, QK_PAD), BF16),
                   jax.ShapeDtypeStruct((b, s, QK_PAD), BF16),
                   jax.ShapeDtypeStruct((b, s // tm, VT_ROWS, tm), BF16),
                   jax.ShapeDtypeStruct((b, s, xaw), BF16)),
        grid=(b, s // tm),
        in_specs=[pl.BlockSpec((1, tm, d), lambda i, j: (i, j, 0)),
                  _const_spec((1, d)), _const_spec(w["wa"].shape),
                  _const_spec((1, Q_LORA)), _const_spec((1, KV_LORA)),
                  _const_spec(w["wuq"].shape), _const_spec(w["wuk"].shape),
                  rope_spec, rope_spec, mem_spec, mem_spec],
        out_specs=(pl.BlockSpec((1, MLA_HEADS, tm, QK_PAD), lambda i, j: (i, 0, j, 0)),
                   pl.BlockSpec((1, tm, QK_PAD), lambda i, j: (i, j, 0)),
                   pl.BlockSpec((1, 1, VT_ROWS, tm), lambda i, j: (i, j, 0, 0)),
                   pl.BlockSpec((1, tm, xaw), lambda i, j: (i, j, 0))),
        compiler_params=_params("arbitrary", "arbitrary"),
        name="mla_proj",
    )(x, w["norm"], w["wa"], w["qn"], w["kvn"], w["wuq"], w["wuk"], cosp, sinp, memk, memv)


def _attn_kernel(q_ref, k_ref, vt_ref, o_ref, acc_ref, m_ref, s0_ref, s1_ref, cm0_ref, cm1_ref,
                 *, tk, n_chunks):
    heads = range(MLA_HEADS)
    slots = ((s0_ref, cm0_ref), (s1_ref, cm1_ref))
    acc_ref[...] = jnp.zeros(acc_ref.shape, F32)
    m_ref[...] = jnp.full(m_ref.shape, -jnp.inf, F32)

    def scores(c, slot):
        s_ref, cm_ref = slots[slot]
        k = k_ref[0, pl.ds(pl.multiple_of(c * tk, tk), tk), :]
        for hd in heads:
            s = _dot_nt(k, q_ref[0, hd])
            s_ref[hd] = s
            cm_ref[hd] = jnp.max(s, axis=0, keepdims=True)

    def update(c, slot):
        s_ref, cm_ref = slots[slot]
        vt = vt_ref[0, c]
        m_old = [m_ref[hd] for hd in heads]
        m_new = [jnp.maximum(m_old[hd], cm_ref[hd]) for hd in heads]
        ps = [jnp.exp2(s_ref[hd] - m_new[hd]).astype(BF16) for hd in heads]
        pvs = [_dot(vt, ps[hd]) for hd in heads]
        for hd in heads:
            acc_ref[hd] = jnp.exp2(m_old[hd] - m_new[hd]) * acc_ref[hd] + pvs[hd]
            m_ref[hd] = m_new[hd]

    def fused(c_sc, slot_sc, c_up, slot_up):
        s_ref, cm_ref = slots[slot_sc]
        u_ref, ucm_ref = slots[slot_up]
        k = k_ref[0, pl.ds(pl.multiple_of(c_sc * tk, tk), tk), :]
        vt = vt_ref[0, c_up]
        for hd in heads:
            s = _dot_nt(k, q_ref[0, hd])
            s_ref[hd] = s
            cm_ref[hd] = jnp.max(s, axis=0, keepdims=True)
            m_old = m_ref[hd]
            m_new = jnp.maximum(m_old, ucm_ref[hd])
            p = jnp.exp2(u_ref[hd] - m_new).astype(BF16)
            acc_ref[hd] = jnp.exp2(m_old - m_new) * acc_ref[hd] + _dot(vt, p)
            m_ref[hd] = m_new

    def steps(c, count):
        for t in range(count):
            fused(c + t + 1, (t + 1) % 2, c + t, t % 2)

    def body(i, carry):
        steps(ATTN_UNROLL * i, ATTN_UNROLL)
        return carry

    scores(0, 0)
    lax.fori_loop(0, n_chunks // ATTN_UNROLL - 1, body, 0)
    steps(n_chunks - ATTN_UNROLL, ATTN_UNROLL - 1)
    update(n_chunks - 1, 1)
    for hd in heads:
        acc = acc_ref[hd]
        o = acc[:KV_LORA] / acc[KV_LORA:KV_LORA + 1]
        o_ref[0, :, hd * KV_LORA:(hd + 1) * KV_LORA] = o.T.astype(BF16)


def _attention(q, k, vt):
    b, nh, s, dk = q.shape
    tq = ATTN_Q_TILE
    n_chunks, vt_rows, tk = vt.shape[1:]
    assert n_chunks % ATTN_UNROLL == 0, n_chunks
    return pl.pallas_call(
        functools.partial(_attn_kernel, tk=tk, n_chunks=n_chunks),
        out_shape=jax.ShapeDtypeStruct((b, s, nh * KV_LORA), BF16),
        grid=(b, s // tq),
        in_specs=[pl.BlockSpec((1, nh, tq, dk), lambda i, j: (i, 0, j, 0)),
                  pl.BlockSpec((1, s, dk), lambda i, j: (i, 0, 0), pipeline_mode=pl.Buffered(1)),
                  pl.BlockSpec((1, n_chunks, vt_rows, tk), lambda i, j: (i, 0, 0, 0),
                               pipeline_mode=pl.Buffered(1))],
        out_specs=pl.BlockSpec((1, tq, nh * KV_LORA), lambda i, j: (i, j, 0)),
        scratch_shapes=[pltpu.VMEM((nh, vt_rows, tq), F32), pltpu.VMEM((nh, 1, tq), F32),
                        pltpu.VMEM((nh, tk, tq), F32), pltpu.VMEM((nh, tk, tq), F32),
                        pltpu.VMEM((nh, 1, tq), F32), pltpu.VMEM((nh, 1, tq), F32)],
        compiler_params=_params("arbitrary", "arbitrary"),
        name="mla_attention",
    )(q, k, vt)


def _mla_out_kernel(x_ref, ol_ref, xa_ref, wuv_ref, woo_ref, wox_ref, o_ref, o_scr):
    for pair in range(MLA_HEADS // 2):
        sl = slice(pair * 2 * KV_LORA, (pair + 1) * 2 * KV_LORA)
        o_scr[:, sl] = _dot(ol_ref[0, :, sl], wuv_ref[pair]).astype(BF16)
    o_ref[0] = x_ref[0] + _dot(o_scr[...], woo_ref[...]) + _dot(xa_ref[0], wox_ref[...])


def _mla_out(x, o_lat, xa, w):
    b, s, d = x.shape
    tm = TOKEN_TILE
    ow = o_lat.shape[2]
    xaw = xa.shape[2]
    return pl.pallas_call(
        _mla_out_kernel,
        out_shape=jax.ShapeDtypeStruct((b, s, d), F32),
        grid=(b, s // tm),
        in_specs=[pl.BlockSpec((1, tm, d), lambda i, j: (i, j, 0)),
                  pl.BlockSpec((1, tm, ow), lambda i, j: (i, j, 0)),
                  pl.BlockSpec((1, tm, xaw), lambda i, j: (i, j, 0)),
                  _const_spec(w["wuv"].shape), _const_spec(w["woo"].shape),
                  _const_spec(w["wox"].shape)],
        out_specs=pl.BlockSpec((1, tm, d), lambda i, j: (i, j, 0)),
        scratch_shapes=[pltpu.VMEM((tm, MLA_HEADS * V_HEAD), BF16)],
        compiler_params=_params("arbitrary", "arbitrary"),
        name="mla_out",
    )(x, o_lat, xa, w["wuv"], w["woo"], w["wox"])


def _block_diag_pairs(w):
    n, a, b = w.shape
    z = jnp.zeros((n // 2, a, b), w.dtype)
    top = jnp.concatenate([w[0::2], z], axis=2)
    bot = jnp.concatenate([z, w[1::2]], axis=2)
    return jnp.concatenate([top, bot], axis=1)


def _rope_cols(w1, w2):
    z = jnp.zeros(w1.shape[:-1] + (128 - QK_ROPE,), w1.dtype)
    return jnp.concatenate([w1, w2, z], axis=-1), jnp.concatenate([w2, w1, z], axis=-1)


def _prep_sgu(j, p):
    sw = SGU_GROUPS * SGU_GROUP_DIM
    bs = jnp.broadcast_to(p["sgu_b_s"][j].T[:, :, None], (CHUNK, SGU_GROUPS, SGU_GROUP_DIM))
    return {
        "win": p["sgu_w_in"][j].astype(BF16),
        "vn": p["sgu_v_norm"][j][None, :],
        "ws": p["sgu_w_s"][j].astype(BF16),
        "bs": bs.reshape(CHUNK, sw),
        "wout": p["sgu_w_out"][j].astype(BF16),
    }


def _prep_mla(j, p):
    w_in = p["mla_w_in"][j]
    o1 = Q_LORA
    o2 = o1 + KV_LORA
    o3 = o2 + QK_ROPE
    kra, krb = _rope_cols(w_in[:, o2:o2 + HALF_ROPE], w_in[:, o2 + HALF_ROPE:o3])
    wa = jnp.concatenate([w_in[:, :o2], kra, krb, w_in[:, o3:]], axis=1)
    wuq = p["mla_w_uq"][j].reshape(Q_LORA, MLA_HEADS, QK_NOPE + QK_ROPE)
    qa, qb = _rope_cols(wuq[:, :, QK_NOPE:QK_NOPE + HALF_ROPE], wuq[:, :, QK_NOPE + HALF_ROPE:])
    wuq_all = jnp.concatenate([wuq[:, :, :QK_NOPE].reshape(Q_LORA, -1),
                               qa.reshape(Q_LORA, -1), qb.reshape(Q_LORA, -1)], axis=1)
    wuk = jnp.transpose(p["mla_w_uk"][j], (1, 2, 0))
    wuv = jnp.transpose(p["mla_w_uv"][j], (1, 0, 2))
    w_out = p["mla_w_out"][j]
    ow = MLA_HEADS * V_HEAD
    return {
        "wa": wa.astype(BF16),
        "qn": p["mla_q_norm"][j][None, :],
        "kvn": p["mla_kv_norm"][j][None, :],
        "wuq": wuq_all.astype(BF16),
        "wuk": _block_diag_pairs(wuk).astype(BF16),
        "wuv": _block_diag_pairs(wuv).astype(BF16),
        "woo": w_out[:ow].astype(BF16),
        "wox": w_out[ow:].astype(BF16),
    }


def _rope_tables(seq_len):
    inv_freq = 1.0 / (ROPE_THETA ** (jnp.arange(0, QK_ROPE, 2, dtype=F32) / QK_ROPE))
    ang = jnp.arange(seq_len, dtype=F32)[:, None] * inv_freq[None, :]
    cos, sin = jnp.cos(ang), jnp.sin(ang)
    z = jnp.zeros((seq_len, 128 - QK_ROPE), F32)
    return (jnp.concatenate([cos, cos, z], axis=1), jnp.concatenate([-sin, sin, z], axis=1))


def _trunk(x, mem, p, prepped):
    b, s, d = x.shape
    depth = p["ffn1_w_in"].shape[0]
    memk, memv = _memkv(mem, p["mem_norm"][:, None, :], prepped["w_mem_kv"])
    cosp, sinp = _rope_tables(s)
    for i in range(depth):
        j = i // 2
        x = _ffn(x.reshape(b * s, d), p["ffn1_norm"][i][None, :],
                 prepped["ffn1_w_in"][i], prepped["ffn1_w_out"][i]).reshape(b, s, d)
        if i % 2 == 0:
            w = dict(prepped["sgu"][j], norm=p["mix_norm"][i][None, :])
            x = _sgu(x, w, memk, memv, i)
        else:
            w = dict(prepped["mla"][j], norm=p["mix_norm"][i][None, :])
            q, k, vt, xa = _mla_proj(x, w, cosp, sinp, memk, memv, i)
            o_lat = _attention(q, k, vt)
            x = _mla_out(x, o_lat, xa, w)
        final_g = p["final_norm"][None, :] if i == depth - 1 else None
        x = _ffn(x.reshape(b * s, d), p["ffn2_norm"][i][None, :],
                 prepped["ffn2_w_in"][i], prepped["ffn2_w_out"][i], final_g).reshape(b, s, d)
    return x


def _prepare(p):
    return {
        "ffn1_w_in": p["ffn1_w_in"].astype(BF16), "ffn1_w_out": p["ffn1_w_out"].astype(BF16),
        "ffn2_w_in": p["ffn2_w_in"].astype(BF16), "ffn2_w_out": p["ffn2_w_out"].astype(BF16),
        "w_mem_kv": p["w_mem_kv"].astype(BF16),
        "sgu": [_prep_sgu(j, p) for j in range(p["sgu_w_in"].shape[0])],
        "mla": [_prep_mla(j, p) for j in range(p["mla_w_in"].shape[0])],
    }


def kernel(x_prompt, x_sample, mem_prompt, mem_sample, ffn1_norm, ffn1_w_in, ffn1_w_out, mix_norm, mem_norm, w_mem_kv, ffn2_norm, ffn2_w_in, ffn2_w_out, sgu_w_in, sgu_v_norm, sgu_w_s, sgu_b_s, sgu_w_out, mla_w_in, mla_q_norm, mla_w_uq, mla_kv_norm, mla_w_uk, mla_w_uv, mla_w_out, final_norm):
    p = {
        'ffn1_norm': ffn1_norm, 'ffn1_w_in': ffn1_w_in, 'ffn1_w_out': ffn1_w_out,
        'mix_norm': mix_norm, 'mem_norm': mem_norm, 'w_mem_kv': w_mem_kv,
        'ffn2_norm': ffn2_norm, 'ffn2_w_in': ffn2_w_in, 'ffn2_w_out': ffn2_w_out,
        'sgu_w_in': sgu_w_in, 'sgu_v_norm': sgu_v_norm, 'sgu_w_s': sgu_w_s,
        'sgu_b_s': sgu_b_s, 'sgu_w_out': sgu_w_out,
        'mla_w_in': mla_w_in, 'mla_q_norm': mla_q_norm, 'mla_w_uq': mla_w_uq,
        'mla_kv_norm': mla_kv_norm, 'mla_w_uk': mla_w_uk, 'mla_w_uv': mla_w_uv,
        'mla_w_out': mla_w_out, 'final_norm': final_norm,
    }
    prepped = _prepare(p)
    return (_trunk(x_prompt, mem_prompt, p, prepped), _trunk(x_sample, mem_sample, p, prepped))
```

```python
import functools

import jax
import jax.numpy as jnp
from jax import lax
from jax.experimental import pallas as pl
from jax.experimental.pallas import tpu as pltpu

F32 = jnp.float32
BF16 = jnp.bfloat16

NORM_EPS = 1e-6
ROPE_THETA = 10000.0
XA_HEADS = 4
XA_HEAD_DIM = 128
SGU_GROUPS = 8
SGU_GROUP_DIM = 192
CHUNK = 128
MLA_HEADS = 8
Q_LORA = 256
KV_LORA = 128
QK_NOPE = 128
QK_ROPE = 64
HALF_ROPE = QK_ROPE // 2
V_HEAD = 128
QK_PAD = 256
VT_ROWS = KV_LORA + 16
LOG2_E = 1.4426950408889634

VMEM_LIMIT_BYTES = 56 * 1024 * 1024
TOKEN_TILE = 512
ATTN_Q_TILE = 256
ATTN_KV_SUB = 2
ATTN_UNROLL = 2


def _params(*semantics):
    return pltpu.CompilerParams(dimension_semantics=semantics,
                                vmem_limit_bytes=VMEM_LIMIT_BYTES)


def _const_spec(shape):
    nd = len(shape)
    return pl.BlockSpec(shape, lambda *_: (0,) * nd, pipeline_mode=pl.Buffered(1))


def _rms(x, g):
    return x * lax.rsqrt(jnp.mean(x * x, axis=-1, keepdims=True) + NORM_EPS) * g


def _gelu_tanh(x):
    return 0.5 * x * (1.0 + jnp.tanh(0.7978845608028654 * (x + 0.044715 * (x * x * x))))


def _dot(a, b):
    return jnp.dot(a, b, preferred_element_type=F32)


def _dot_nt(a, b):
    return lax.dot_general(a, b, (((1,), (1,)), ((), ())), preferred_element_type=F32)


def _ffn_kernel(x_ref, g_ref, win_ref, wout_ref, *rest, d_ff, final):
    if final:
        fg_ref, o_ref = rest
    else:
        (o_ref,) = rest
    x = x_ref[...]
    h = _rms(x, g_ref[...]).astype(BF16)
    gu = _dot(h, win_ref[...])
    gate = gu[:, :d_ff]
    up = gu[:, d_ff:]
    act = (gate * (1.0 / (1.0 + jnp.exp(-gate))) * up).astype(BF16)
    y = x + 0.5 * _dot(act, wout_ref[...])
    if final:
        y = _rms(y, fg_ref[...])
    o_ref[...] = y


def _ffn(x2, g, w_in, w_out, final_g=None):
    n, d = x2.shape
    d_ff = w_out.shape[0]
    tm = TOKEN_TILE
    final = final_g is not None
    in_specs = [pl.BlockSpec((tm, d), lambda i: (i, 0)),
                _const_spec((1, d)), _const_spec(w_in.shape), _const_spec(w_out.shape)]
    args = [x2, g, w_in, w_out]
    if final:
        in_specs.append(_const_spec((1, d)))
        args.append(final_g)
    return pl.pallas_call(
        functools.partial(_ffn_kernel, d_ff=d_ff, final=final),
        out_shape=jax.ShapeDtypeStruct((n, d), F32),
        grid=(n // tm,),
        in_specs=in_specs,
        out_specs=pl.BlockSpec((tm, d), lambda i: (i, 0)),
        compiler_params=_params("arbitrary"),
        name="ffn_final" if final else "ffn",
    )(*args)


def _memkv_kernel(mem_ref, g_ref, w_ref, k_ref, v_ref, *, width):
    h = _rms(mem_ref[0], g_ref[0]).astype(BF16)
    kv = _dot(h, w_ref[0])
    k_ref[0, 0] = kv[:, :width].astype(BF16)
    v_ref[0, 0] = kv[:, width:].astype(BF16)


def _memkv(mem, g, w):
    b, m, d = mem.shape
    nl = w.shape[0]
    width = w.shape[2] // 2
    out = jax.ShapeDtypeStruct((nl, b, m, width), BF16)
    return pl.pallas_call(
        functools.partial(_memkv_kernel, width=width),
        out_shape=(out, out),
        grid=(nl, b),
        in_specs=[pl.BlockSpec((1, m, d), lambda l, i: (i, 0, 0)),
                  pl.BlockSpec((1, 1, d), lambda l, i: (l, 0, 0)),
                  pl.BlockSpec((1, d, 2 * width), lambda l, i: (l, 0, 0))],
        out_specs=(pl.BlockSpec((1, 1, m, width), lambda l, i: (l, i, 0, 0)),
                   pl.BlockSpec((1, 1, m, width), lambda l, i: (l, i, 0, 0))),
        compiler_params=_params("arbitrary", "arbitrary"),
        name="memkv",
    )(mem, g, w)


def _memory_attention(qm, mk_ref, mv_ref):
    outs = []
    for hh in range(XA_HEADS):
        sl = slice(hh * XA_HEAD_DIM, (hh + 1) * XA_HEAD_DIM)
        s = _dot_nt(qm[:, sl].astype(BF16), mk_ref[0, 0, :, sl])
        p = jnp.exp(s - jnp.max(s, axis=-1, keepdims=True))
        l = jnp.sum(p, axis=-1, keepdims=True)
        outs.append(_dot(p.astype(BF16), mv_ref[0, 0, :, sl]) / l)
    return outs


def _sgu_kernel(x_ref, g_ref, win_ref, vn_ref, ws_ref, bs_ref, mk_ref, mv_ref, wout_ref,
                o_ref, cat_ref, *, tm):
    sw = SGU_GROUPS * SGU_GROUP_DIM
    pair_w = 2 * SGU_GROUP_DIM
    x = x_ref[0]
    h = _rms(x, g_ref[...]).astype(BF16)
    proj = _dot(h, win_ref[...])
    v = _gelu_tanh(proj[:, sw:2 * sw])
    ms = jnp.mean(v * v, axis=-1, keepdims=True)
    vn = (v * lax.rsqrt(ms + NORM_EPS) * vn_ref[...]).astype(BF16)
    first_half = lax.broadcasted_iota(jnp.int32, (CHUNK, 128), 1) < (SGU_GROUP_DIM - 128)
    for n in range(tm // CHUNK):
        rows = slice(n * CHUNK, (n + 1) * CHUNK)
        for j in range(SGU_GROUPS // 2):
            c0 = j * pair_w
            r0 = _dot(ws_ref[2 * j], vn[rows, c0:c0 + 256])
            r1 = _dot(ws_ref[2 * j + 1], vn[rows, c0 + 128:c0 + 384])
            mixed = (r0[:, :128], jnp.where(first_half, r0[:, 128:], r1[:, :128]), r1[:, 128:])
            for t in range(3):
                cols = slice(c0 + t * 128, c0 + (t + 1) * 128)
                u = _gelu_tanh(proj[rows, cols])
                cat_ref[rows, cols] = (u * (mixed[t] + bs_ref[:, cols])).astype(BF16)
    qm = proj[:, 2 * sw:] * (XA_HEAD_DIM ** -0.5)
    for hh, oh in enumerate(_memory_attention(qm, mk_ref, mv_ref)):
        cat_ref[:, sw + hh * XA_HEAD_DIM:sw + (hh + 1) * XA_HEAD_DIM] = oh.astype(BF16)
    o_ref[0] = x + _dot(cat_ref[...], wout_ref[...])


def _sgu(x, w, memk, memv, layer):
    b, s, d = x.shape
    tm = TOKEN_TILE
    sw = SGU_GROUPS * SGU_GROUP_DIM
    xaw = XA_HEADS * XA_HEAD_DIM
    m = memk.shape[2]
    mem_spec = pl.BlockSpec((1, 1, m, xaw), lambda i, j: (layer, i, 0, 0))
    return pl.pallas_call(
        functools.partial(_sgu_kernel, tm=tm),
        out_shape=jax.ShapeDtypeStruct((b, s, d), F32),
        grid=(b, s // tm),
        in_specs=[pl.BlockSpec((1, tm, d), lambda i, j: (i, j, 0)),
                  _const_spec((1, d)), _const_spec(w["win"].shape), _const_spec((1, sw)),
                  _const_spec(w["ws"].shape), _const_spec(w["bs"].shape),
                  mem_spec, mem_spec, _const_spec(w["wout"].shape)],
        out_specs=pl.BlockSpec((1, tm, d), lambda i, j: (i, j, 0)),
        scratch_shapes=[pltpu.VMEM((tm, sw + xaw), BF16)],
        compiler_params=_params("arbitrary", "arbitrary"),
        name="sgu_mixer",
    )(x, w["norm"], w["win"], w["vn"], w["ws"], w["bs"], memk, memv, w["wout"])


def _mla_proj_kernel(x_ref, g_ref, wa_ref, qn_ref, kvn_ref, wuq_ref, wuk_ref, cos_ref, sin_ref,
                     mk_ref, mv_ref, q_ref, k_ref, vt_ref, xa_ref):
    x = x_ref[0]
    h = _rms(x, g_ref[...]).astype(BF16)
    proj = _dot(h, wa_ref[...])
    o1 = Q_LORA
    o2 = o1 + KV_LORA
    o3 = o2 + 128
    o4 = o3 + 128
    cos = cos_ref[...]
    sin = sin_ref[...]
    c_q = _rms(proj[:, :o1], qn_ref[...]).astype(BF16)
    c_kv = _rms(proj[:, o1:o2], kvn_ref[...])
    k_rope = proj[:, o2:o3] * cos + proj[:, o3:o4] * sin
    k_ref[0, :, :KV_LORA] = c_kv.astype(BF16)
    k_ref[0, :, KV_LORA:] = k_rope.astype(BF16)
    vt_ref[0, 0, :KV_LORA, :] = c_kv.T.astype(BF16)
    vt_ref[0, 0, KV_LORA:, :] = jnp.ones((VT_ROWS - KV_LORA, c_kv.shape[0]), BF16)

    q = _dot(c_q, wuq_ref[...])
    hw = MLA_HEADS * QK_NOPE
    scale = (QK_NOPE + QK_ROPE) ** -0.5 * LOG2_E
    for pair in range(MLA_HEADS // 2):
        sl = slice(pair * 2 * QK_NOPE, (pair + 1) * 2 * QK_NOPE)
        q_lat = _dot(q[:, sl].astype(BF16), wuk_ref[pair])
        for t in range(2):
            hd = pair * 2 + t
            q_rope = q[:, hw + hd * 128:hw + (hd + 1) * 128] * cos \
                + q[:, 2 * hw + hd * 128:2 * hw + (hd + 1) * 128] * sin
            q_ref[0, hd, :, :KV_LORA] = (q_lat[:, t * KV_LORA:(t + 1) * KV_LORA] * scale).astype(BF16)
            q_ref[0, hd, :, KV_LORA:] = (q_rope * scale).astype(BF16)

    qm = proj[:, o4:] * (XA_HEAD_DIM ** -0.5)
    for hh, oh in enumerate(_memory_attention(qm, mk_ref, mv_ref)):
        xa_ref[0, :, hh * XA_HEAD_DIM:(hh + 1) * XA_HEAD_DIM] = oh.astype(BF16)


def _mla_proj(x, w, cosp, sinp, memk, memv, layer):
    b, s, d = x.shape
    tm = TOKEN_TILE
    xaw = XA_HEADS * XA_HEAD_DIM
    m = memk.shape[2]
    mem_spec = pl.BlockSpec((1, 1, m, xaw), lambda i, j: (layer, i, 0, 0))
    rope_spec = pl.BlockSpec((tm, 128), lambda i, j: (j, 0))
    return pl.pallas_call(
        _mla_proj_kernel,
        out_shape=(jax.ShapeDtypeStruct((b, MLA_HEADS, s, QK_PAD), BF16),
                   jax.ShapeDtypeStruct((b, s, QK_PAD), BF16),
                   jax.ShapeDtypeStruct((b, s // tm, VT_ROWS, tm), BF16),
                   jax.ShapeDtypeStruct((b, s, xaw), BF16)),
        grid=(b, s // tm),
        in_specs=[pl.BlockSpec((1, tm, d), lambda i, j: (i, j, 0)),
                  _const_spec((1, d)), _const_spec(w["wa"].shape),
                  _const_spec((1, Q_LORA)), _const_spec((1, KV_LORA)),
                  _const_spec(w["wuq"].shape), _const_spec(w["wuk"].shape),
                  rope_spec, rope_spec, mem_spec, mem_spec],
        out_specs=(pl.BlockSpec((1, MLA_HEADS, tm, QK_PAD), lambda i, j: (i, 0, j, 0)),
                   pl.BlockSpec((1, tm, QK_PAD), lambda i, j: (i, j, 0)),
                   pl.BlockSpec((1, 1, VT_ROWS, tm), lambda i, j: (i, j, 0, 0)),
                   pl.BlockSpec((1, tm, xaw), lambda i, j: (i, j, 0))),
        compiler_params=_params("arbitrary", "arbitrary"),
        name="mla_proj",
    )(x, w["norm"], w["wa"], w["qn"], w["kvn"], w["wuq"], w["wuk"], cosp, sinp, memk, memv)


def _attn_kernel(q_ref, k_ref, vt_ref, o_ref, acc_ref, m_ref, s0_ref, s1_ref, cm0_ref, cm1_ref,
                 *, tk, n_chunks):
    heads = range(MLA_HEADS)
    slots = ((s0_ref, cm0_ref), (s1_ref, cm1_ref))
    acc_ref[...] = jnp.zeros(acc_ref.shape, F32)
    m_ref[...] = jnp.full(m_ref.shape, -jnp.inf, F32)

    def scores(c, slot):
        s_ref, cm_ref = slots[slot]
        k = k_ref[0, pl.ds(pl.multiple_of(c * tk, tk), tk), :]
        for hd in heads:
            s = _dot_nt(k, q_ref[0, hd])
            s_ref[hd] = s
            cm_ref[hd] = jnp.max(s, axis=0, keepdims=True)

    def values(c, p):
        vtw = vt_ref.shape[3]
        return sum(_dot(vt_ref[0, c * ATTN_KV_SUB + j], p[j * vtw:(j + 1) * vtw])
                   for j in range(ATTN_KV_SUB))

    def update(c, slot):
        s_ref, cm_ref = slots[slot]
        m_old = [m_ref[hd] for hd in heads]
        m_new = [jnp.maximum(m_old[hd], cm_ref[hd]) for hd in heads]
        ps = [jnp.exp2(s_ref[hd] - m_new[hd]).astype(BF16) for hd in heads]
        pvs = [values(c, ps[hd]) for hd in heads]
        for hd in heads:
            acc_ref[hd] = jnp.exp2(m_old[hd] - m_new[hd]) * acc_ref[hd] + pvs[hd]
            m_ref[hd] = m_new[hd]

    def fused(c_sc, slot_sc, c_up, slot_up):
        s_ref, cm_ref = slots[slot_sc]
        u_ref, ucm_ref = slots[slot_up]
        k = k_ref[0, pl.ds(pl.multiple_of(c_sc * tk, tk), tk), :]
        for hd in heads:
            s = _dot_nt(k, q_ref[0, hd])
            s_ref[hd] = s
            cm_ref[hd] = jnp.max(s, axis=0, keepdims=True)
            m_old = m_ref[hd]
            m_new = jnp.maximum(m_old, ucm_ref[hd])
            p = jnp.exp2(u_ref[hd] - m_new).astype(BF16)
            acc_ref[hd] = jnp.exp2(m_old - m_new) * acc_ref[hd] + values(c_up, p)
            m_ref[hd] = m_new

    def steps(c, count):
        for t in range(count):
            fused(c + t + 1, (t + 1) % 2, c + t, t % 2)

    def body(i, carry):
        steps(ATTN_UNROLL * i, ATTN_UNROLL)
        return carry

    scores(0, 0)
    lax.fori_loop(0, n_chunks // ATTN_UNROLL - 1, body, 0)
    steps(n_chunks - ATTN_UNROLL, ATTN_UNROLL - 1)
    update(n_chunks - 1, 1)
    for hd in heads:
        acc = acc_ref[hd]
        o = acc[:KV_LORA] / acc[KV_LORA:KV_LORA + 1]
        o_ref[0, :, hd * KV_LORA:(hd + 1) * KV_LORA] = o.T.astype(BF16)


def _attention(q, k, vt):
    b, nh, s, dk = q.shape
    tq = ATTN_Q_TILE
    n_vt, vt_rows, vtw = vt.shape[1:]
    tk = vtw * ATTN_KV_SUB
    n_chunks = s // tk
    assert n_chunks % ATTN_UNROLL == 0, n_chunks
    return pl.pallas_call(
        functools.partial(_attn_kernel, tk=tk, n_chunks=n_chunks),
        out_shape=jax.ShapeDtypeStruct((b, s, nh * KV_LORA), BF16),
        grid=(b, s // tq),
        in_specs=[pl.BlockSpec((1, nh, tq, dk), lambda i, j: (i, 0, j, 0)),
                  pl.BlockSpec((1, s, dk), lambda i, j: (i, 0, 0), pipeline_mode=pl.Buffered(1)),
                  pl.BlockSpec((1, n_vt, vt_rows, vtw), lambda i, j: (i, 0, 0, 0),
                               pipeline_mode=pl.Buffered(1))],
        out_specs=pl.BlockSpec((1, tq, nh * KV_LORA), lambda i, j: (i, j, 0)),
        scratch_shapes=[pltpu.VMEM((nh, vt_rows, tq), F32), pltpu.VMEM((nh, 1, tq), F32),
                        pltpu.VMEM((nh, tk, tq), F32), pltpu.VMEM((nh, tk, tq), F32),
                        pltpu.VMEM((nh, 1, tq), F32), pltpu.VMEM((nh, 1, tq), F32)],
        compiler_params=_params("arbitrary", "arbitrary"),
        name="mla_attention",
    )(q, k, vt)


def _mla_out_kernel(x_ref, ol_ref, xa_ref, wuv_ref, woo_ref, wox_ref, o_ref, o_scr):
    for pair in range(MLA_HEADS // 2):
        sl = slice(pair * 2 * KV_LORA, (pair + 1) * 2 * KV_LORA)
        o_scr[:, sl] = _dot(ol_ref[0, :, sl], wuv_ref[pair]).astype(BF16)
    o_ref[0] = x_ref[0] + _dot(o_scr[...], woo_ref[...]) + _dot(xa_ref[0], wox_ref[...])


def _mla_out(x, o_lat, xa, w):
    b, s, d = x.shape
    tm = TOKEN_TILE
    ow = o_lat.shape[2]
    xaw = xa.shape[2]
    return pl.pallas_call(
        _mla_out_kernel,
        out_shape=jax.ShapeDtypeStruct((b, s, d), F32),
        grid=(b, s // tm),
        in_specs=[pl.BlockSpec((1, tm, d), lambda i, j: (i, j, 0)),
                  pl.BlockSpec((1, tm, ow), lambda i, j: (i, j, 0)),
                  pl.BlockSpec((1, tm, xaw), lambda i, j: (i, j, 0)),
                  _const_spec(w["wuv"].shape), _const_spec(w["woo"].shape),
                  _const_spec(w["wox"].shape)],
        out_specs=pl.BlockSpec((1, tm, d), lambda i, j: (i, j, 0)),
        scratch_shapes=[pltpu.VMEM((tm, MLA_HEADS * V_HEAD), BF16)],
        compiler_params=_params("arbitrary", "arbitrary"),
        name="mla_out",
    )(x, o_lat, xa, w["wuv"], w["woo"], w["wox"])


def _block_diag_pairs(w):
    n, a, b = w.shape
    z = jnp.zeros((n // 2, a, b), w.dtype)
    top = jnp.concatenate([w[0::2], z], axis=2)
    bot = jnp.concatenate([z, w[1::2]], axis=2)
    return jnp.concatenate([top, bot], axis=1)


def _rope_cols(w1, w2):
    z = jnp.zeros(w1.shape[:-1] + (128 - QK_ROPE,), w1.dtype)
    return jnp.concatenate([w1, w2, z], axis=-1), jnp.concatenate([w2, w1, z], axis=-1)


def _prep_sgu(j, p):
    sw = SGU_GROUPS * SGU_GROUP_DIM
    bs = jnp.broadcast_to(p["sgu_b_s"][j].T[:, :, None], (CHUNK, SGU_GROUPS, SGU_GROUP_DIM))
    return {
        "win": p["sgu_w_in"][j].astype(BF16),
        "vn": p["sgu_v_norm"][j][None, :],
        "ws": p["sgu_w_s"][j].astype(BF16),
        "bs": bs.reshape(CHUNK, sw),
        "wout": p["sgu_w_out"][j].astype(BF16),
    }


def _prep_mla(j, p):
    w_in = p["mla_w_in"][j]
    o1 = Q_LORA
    o2 = o1 + KV_LORA
    o3 = o2 + QK_ROPE
    kra, krb = _rope_cols(w_in[:, o2:o2 + HALF_ROPE], w_in[:, o2 + HALF_ROPE:o3])
    wa = jnp.concatenate([w_in[:, :o2], kra, krb, w_in[:, o3:]], axis=1)
    wuq = p["mla_w_uq"][j].reshape(Q_LORA, MLA_HEADS, QK_NOPE + QK_ROPE)
    qa, qb = _rope_cols(wuq[:, :, QK_NOPE:QK_NOPE + HALF_ROPE], wuq[:, :, QK_NOPE + HALF_ROPE:])
    wuq_all = jnp.concatenate([wuq[:, :, :QK_NOPE].reshape(Q_LORA, -1),
                               qa.reshape(Q_LORA, -1), qb.reshape(Q_LORA, -1)], axis=1)
    wuk = jnp.transpose(p["mla_w_uk"][j], (1, 2, 0))
    wuv = jnp.transpose(p["mla_w_uv"][j], (1, 0, 2))
    w_out = p["mla_w_out"][j]
    ow = MLA_HEADS * V_HEAD
    return {
        "wa": wa.astype(BF16),
        "qn": p["mla_q_norm"][j][None, :],
        "kvn": p["mla_kv_norm"][j][None, :],
        "wuq": wuq_all.astype(BF16),
        "wuk": _block_diag_pairs(wuk).astype(BF16),
        "wuv": _block_diag_pairs(wuv).astype(BF16),
        "woo": w_out[:ow].astype(BF16),
        "wox": w_out[ow:].astype(BF16),
    }


def _rope_tables(seq_len):
    inv_freq = 1.0 / (ROPE_THETA ** (jnp.arange(0, QK_ROPE, 2, dtype=F32) / QK_ROPE))
    ang = jnp.arange(seq_len, dtype=F32)[:, None] * inv_freq[None, :]
    cos, sin = jnp.cos(ang), jnp.sin(ang)
    z = jnp.zeros((seq_len, 128 - QK_ROPE), F32)
    return (jnp.concatenate([cos, cos, z], axis=1), jnp.concatenate([-sin, sin, z], axis=1))


def _trunk(x, mem, p, prepped):
    b, s, d = x.shape
    depth = p["ffn1_w_in"].shape[0]
    memk, memv = _memkv(mem, p["mem_norm"][:, None, :], prepped["w_mem_kv"])
    cosp, sinp = _rope_tables(s)
    for i in range(depth):
        j = i // 2
        x = _ffn(x.reshape(b * s, d), p["ffn1_norm"][i][None, :],
                 prepped["ffn1_w_in"][i], prepped["ffn1_w_out"][i]).reshape(b, s, d)
        if i % 2 == 0:
            w = dict(prepped["sgu"][j], norm=p["mix_norm"][i][None, :])
            x = _sgu(x, w, memk, memv, i)
        else:
            w = dict(prepped["mla"][j], norm=p["mix_norm"][i][None, :])
            q, k, vt, xa = _mla_proj(x, w, cosp, sinp, memk, memv, i)
            o_lat = _attention(q, k, vt)
            x = _mla_out(x, o_lat, xa, w)
        final_g = p["final_norm"][None, :] if i == depth - 1 else None
        x = _ffn(x.reshape(b * s, d), p["ffn2_norm"][i][None, :],
                 prepped["ffn2_w_in"][i], prepped["ffn2_w_out"][i], final_g).reshape(b, s, d)
    return x


def _prepare(p):
    return {
        "ffn1_w_in": p["ffn1_w_in"].astype(BF16), "ffn1_w_out": p["ffn1_w_out"].astype(BF16),
        "ffn2_w_in": p["ffn2_w_in"].astype(BF16), "ffn2_w_out": p["ffn2_w_out"].astype(BF16),
        "w_mem_kv": p["w_mem_kv"].astype(BF16),
        "sgu": [_prep_sgu(j, p) for j in range(p["sgu_w_in"].shape[0])],
        "mla": [_prep_mla(j, p) for j in range(p["mla_w_in"].shape[0])],
    }


def kernel(x_prompt, x_sample, mem_prompt, mem_sample, ffn1_norm, ffn1_w_in, ffn1_w_out, mix_norm, mem_norm, w_mem_kv, ffn2_norm, ffn2_w_in, ffn2_w_out, sgu_w_in, sgu_v_norm, sgu_w_s, sgu_b_s, sgu_w_out, mla_w_in, mla_q_norm, mla_w_uq, mla_kv_norm, mla_w_uk, mla_w_uv, mla_w_out, final_norm):
    p = {
        'ffn1_norm': ffn1_norm, 'ffn1_w_in': ffn1_w_in, 'ffn1_w_out': ffn1_w_out,
        'mix_norm': mix_norm, 'mem_norm': mem_norm, 'w_mem_kv': w_mem_kv,
        'ffn2_norm': ffn2_norm, 'ffn2_w_in': ffn2_w_in, 'ffn2_w_out': ffn2_w_out,
        'sgu_w_in': sgu_w_in, 'sgu_v_norm': sgu_v_norm, 'sgu_w_s': sgu_w_s,
        'sgu_b_s': sgu_b_s, 'sgu_w_out': sgu_w_out,
        'mla_w_in': mla_w_in, 'mla_q_norm': mla_q_norm, 'mla_w_uq': mla_w_uq,
        'mla_kv_norm': mla_kv_norm, 'mla_w_uk': mla_w_uk, 'mla_w_uv': mla_w_uv,
        'mla_w_out': mla_w_out, 'final_norm': final_norm,
    }
    prepped = _prepare(p)
    return (_trunk(x_prompt, mem_prompt, p, prepped), _trunk(x_sample, mem_sample, p, prepped))
```

```python
import functools

import jax
import jax.numpy as jnp
from jax import lax
from jax.experimental import pallas as pl
from jax.experimental.pallas import tpu as pltpu

F32 = jnp.float32
BF16 = jnp.bfloat16

NORM_EPS = 1e-6
ROPE_THETA = 10000.0
XA_HEADS = 4
XA_HEAD_DIM = 128
SGU_GROUPS = 8
SGU_GROUP_DIM = 192
CHUNK = 128
MLA_HEADS = 8
Q_LORA = 256
KV_LORA = 128
QK_NOPE = 128
QK_ROPE = 64
HALF_ROPE = QK_ROPE // 2
V_HEAD = 128
QK_PAD = 256
VT_ROWS = KV_LORA + 16
LOG2_E = 1.4426950408889634

VMEM_LIMIT_BYTES = 56 * 1024 * 1024
TOKEN_TILE = 512
ATTN_Q_TILE = 256
ATTN_KV_SUB = 2
ATTN_UNROLL = 2


def _params(*semantics):
    return pltpu.CompilerParams(dimension_semantics=semantics,
                                vmem_limit_bytes=VMEM_LIMIT_BYTES)


def _const_spec(shape):
    nd = len(shape)
    return pl.BlockSpec(shape, lambda *_: (0,) * nd, pipeline_mode=pl.Buffered(1))


def _rms(x, g):
    return x * lax.rsqrt(jnp.mean(x * x, axis=-1, keepdims=True) + NORM_EPS) * g


def _gelu_tanh(x):
    return 0.5 * x * (1.0 + jnp.tanh(0.7978845608028654 * (x + 0.044715 * (x * x * x))))


def _dot(a, b):
    return jnp.dot(a, b, preferred_element_type=F32)


def _dot_nt(a, b):
    return lax.dot_general(a, b, (((1,), (1,)), ((), ())), preferred_element_type=F32)


def _ffn_kernel(x_ref, g_ref, win_ref, wout_ref, *rest, d_ff, final):
    if final:
        fg_ref, o_ref = rest
    else:
        (o_ref,) = rest
    x = x_ref[...]
    h = _rms(x, g_ref[...]).astype(BF16)
    gu = _dot(h, win_ref[...])
    gate = gu[:, :d_ff]
    up = gu[:, d_ff:]
    act = (gate * (1.0 / (1.0 + jnp.exp(-gate))) * up).astype(BF16)
    y = x + 0.5 * _dot(act, wout_ref[...])
    if final:
        y = _rms(y, fg_ref[...])
    o_ref[...] = y


def _ffn(x2, g, w_in, w_out, final_g=None):
    n, d = x2.shape
    d_ff = w_out.shape[0]
    tm = TOKEN_TILE
    final = final_g is not None
    in_specs = [pl.BlockSpec((tm, d), lambda i: (i, 0)),
                _const_spec((1, d)), _const_spec(w_in.shape), _const_spec(w_out.shape)]
    args = [x2, g, w_in, w_out]
    if final:
        in_specs.append(_const_spec((1, d)))
        args.append(final_g)
    return pl.pallas_call(
        functools.partial(_ffn_kernel, d_ff=d_ff, final=final),
        out_shape=jax.ShapeDtypeStruct((n, d), F32),
        grid=(n // tm,),
        in_specs=in_specs,
        out_specs=pl.BlockSpec((tm, d), lambda i: (i, 0)),
        compiler_params=_params("arbitrary"),
        name="ffn_final" if final else "ffn",
    )(*args)


def _memkv_kernel(mem_ref, g_ref, w_ref, k_ref, v_ref, *, width):
    h = _rms(mem_ref[0], g_ref[0]).astype(BF16)
    kv = _dot(h, w_ref[0])
    k_ref[0, 0] = kv[:, :width].astype(BF16)
    v_ref[0, 0] = kv[:, width:].astype(BF16)


def _memkv(mem, g, w):
    b, m, d = mem.shape
    nl = w.shape[0]
    width = w.shape[2] // 2
    out = jax.ShapeDtypeStruct((nl, b, m, width), BF16)
    return pl.pallas_call(
        functools.partial(_memkv_kernel, width=width),
        out_shape=(out, out),
        grid=(nl, b),
        in_specs=[pl.BlockSpec((1, m, d), lambda l, i: (i, 0, 0)),
                  pl.BlockSpec((1, 1, d), lambda l, i: (l, 0, 0)),
                  pl.BlockSpec((1, d, 2 * width), lambda l, i: (l, 0, 0))],
        out_specs=(pl.BlockSpec((1, 1, m, width), lambda l, i: (l, i, 0, 0)),
                   pl.BlockSpec((1, 1, m, width), lambda l, i: (l, i, 0, 0))),
        compiler_params=_params("arbitrary", "arbitrary"),
        name="memkv",
    )(mem, g, w)


def _memory_attention(qm, mk_ref, mv_ref):
    outs = []
    for hh in range(XA_HEADS):
        sl = slice(hh * XA_HEAD_DIM, (hh + 1) * XA_HEAD_DIM)
        s = _dot_nt(qm[:, sl].astype(BF16), mk_ref[0, 0, :, sl])
        p = jnp.exp(s - jnp.max(s, axis=-1, keepdims=True))
        l = jnp.sum(p, axis=-1, keepdims=True)
        outs.append(_dot(p.astype(BF16), mv_ref[0, 0, :, sl]) / l)
    return outs


def _sgu_kernel(x_ref, g_ref, win_ref, vn_ref, ws_ref, bs_ref, mk_ref, mv_ref, wout_ref,
                o_ref, cat_ref, *, tm):
    sw = SGU_GROUPS * SGU_GROUP_DIM
    pair_w = 2 * SGU_GROUP_DIM
    x = x_ref[0]
    h = _rms(x, g_ref[...]).astype(BF16)
    proj = _dot(h, win_ref[...])
    v = _gelu_tanh(proj[:, sw:2 * sw])
    ms = jnp.mean(v * v, axis=-1, keepdims=True)
    vn = (v * lax.rsqrt(ms + NORM_EPS) * vn_ref[...]).astype(BF16)
    first_half = lax.broadcasted_iota(jnp.int32, (CHUNK, 128), 1) < (SGU_GROUP_DIM - 128)
    for n in range(tm // CHUNK):
        rows = slice(n * CHUNK, (n + 1) * CHUNK)
        for j in range(SGU_GROUPS // 2):
            c0 = j * pair_w
            r0 = _dot(ws_ref[2 * j], vn[rows, c0:c0 + 256])
            r1 = _dot(ws_ref[2 * j + 1], vn[rows, c0 + 128:c0 + 384])
            mixed = (r0[:, :128], jnp.where(first_half, r0[:, 128:], r1[:, :128]), r1[:, 128:])
            for t in range(3):
                cols = slice(c0 + t * 128, c0 + (t + 1) * 128)
                u = _gelu_tanh(proj[rows, cols])
                cat_ref[rows, cols] = (u * (mixed[t] + bs_ref[:, cols])).astype(BF16)
    qm = proj[:, 2 * sw:] * (XA_HEAD_DIM ** -0.5)
    for hh, oh in enumerate(_memory_attention(qm, mk_ref, mv_ref)):
        cat_ref[:, sw + hh * XA_HEAD_DIM:sw + (hh + 1) * XA_HEAD_DIM] = oh.astype(BF16)
    o_ref[0] = x + _dot(cat_ref[...], wout_ref[...])


def _sgu(x, w, memk, memv, layer):
    b, s, d = x.shape
    tm = TOKEN_TILE
    sw = SGU_GROUPS * SGU_GROUP_DIM
    xaw = XA_HEADS * XA_HEAD_DIM
    m = memk.shape[2]
    mem_spec = pl.BlockSpec((1, 1, m, xaw), lambda i, j: (layer, i, 0, 0))
    return pl.pallas_call(
        functools.partial(_sgu_kernel, tm=tm),
        out_shape=jax.ShapeDtypeStruct((b, s, d), F32),
        grid=(b, s // tm),
        in_specs=[pl.BlockSpec((1, tm, d), lambda i, j: (i, j, 0)),
                  _const_spec((1, d)), _const_spec(w["win"].shape), _const_spec((1, sw)),
                  _const_spec(w["ws"].shape), _const_spec(w["bs"].shape),
                  mem_spec, mem_spec, _const_spec(w["wout"].shape)],
        out_specs=pl.BlockSpec((1, tm, d), lambda i, j: (i, j, 0)),
        scratch_shapes=[pltpu.VMEM((tm, sw + xaw), BF16)],
        compiler_params=_params("arbitrary", "arbitrary"),
        name="sgu_mixer",
    )(x, w["norm"], w["win"], w["vn"], w["ws"], w["bs"], memk, memv, w["wout"])


def _mla_proj_kernel(x_ref, g_ref, wa_ref, qn_ref, kvn_ref, wuq_ref, wuk_ref, cos_ref, sin_ref,
                     mk_ref, mv_ref, qt_ref, k_ref, vt_ref, xa_ref):
    x = x_ref[0]
    h = _rms(x, g_ref[...]).astype(BF16)
    proj = _dot(h, wa_ref[...])
    o1 = Q_LORA
    o2 = o1 + KV_LORA
    o3 = o2 + 128
    o4 = o3 + 128
    cos = cos_ref[...]
    sin = sin_ref[...]
    c_q = _rms(proj[:, :o1], qn_ref[...]).astype(BF16)
    c_kv = _rms(proj[:, o1:o2], kvn_ref[...])
    k_rope = proj[:, o2:o3] * cos + proj[:, o3:o4] * sin
    k_ref[0, :, :KV_LORA] = c_kv.astype(BF16)
    k_ref[0, :, KV_LORA:] = k_rope.astype(BF16)
    vt_ref[0, 0, :KV_LORA, :] = c_kv.T.astype(BF16)
    vt_ref[0, 0, KV_LORA:, :] = jnp.ones((VT_ROWS - KV_LORA, c_kv.shape[0]), BF16)

    q = _dot(c_q, wuq_ref[...])
    hw = MLA_HEADS * QK_NOPE
    scale = (QK_NOPE + QK_ROPE) ** -0.5 * LOG2_E
    for pair in range(MLA_HEADS // 2):
        sl = slice(pair * 2 * QK_NOPE, (pair + 1) * 2 * QK_NOPE)
        q_lat = _dot(q[:, sl].astype(BF16), wuk_ref[pair])
        for t in range(2):
            hd = pair * 2 + t
            q_rope = q[:, hw + hd * 128:hw + (hd + 1) * 128] * cos \
                + q[:, 2 * hw + hd * 128:2 * hw + (hd + 1) * 128] * sin
            qt_ref[0, hd, :KV_LORA, :] = (q_lat[:, t * KV_LORA:(t + 1) * KV_LORA] * scale).T.astype(BF16)
            qt_ref[0, hd, KV_LORA:, :] = (q_rope * scale).T.astype(BF16)

    qm = proj[:, o4:] * (XA_HEAD_DIM ** -0.5)
    for hh, oh in enumerate(_memory_attention(qm, mk_ref, mv_ref)):
        xa_ref[0, :, hh * XA_HEAD_DIM:(hh + 1) * XA_HEAD_DIM] = oh.astype(BF16)


def _mla_proj(x, w, cosp, sinp, memk, memv, layer):
    b, s, d = x.shape
    tm = TOKEN_TILE
    xaw = XA_HEADS * XA_HEAD_DIM
    m = memk.shape[2]
    mem_spec = pl.BlockSpec((1, 1, m, xaw), lambda i, j: (layer, i, 0, 0))
    rope_spec = pl.BlockSpec((tm, 128), lambda i, j: (j, 0))
    return pl.pallas_call(
        _mla_proj_kernel,
        out_shape=(jax.ShapeDtypeStruct((b, MLA_HEADS, QK_PAD, s), BF16),
                   jax.ShapeDtypeStruct((b, s, QK_PAD), BF16),
                   jax.ShapeDtypeStruct((b, s // tm, VT_ROWS, tm), BF16),
                   jax.ShapeDtypeStruct((b, s, xaw), BF16)),
        grid=(b, s // tm),
        in_specs=[pl.BlockSpec((1, tm, d), lambda i, j: (i, j, 0)),
                  _const_spec((1, d)), _const_spec(w["wa"].shape),
                  _const_spec((1, Q_LORA)), _const_spec((1, KV_LORA)),
                  _const_spec(w["wuq"].shape), _const_spec(w["wuk"].shape),
                  rope_spec, rope_spec, mem_spec, mem_spec],
        out_specs=(pl.BlockSpec((1, MLA_HEADS, QK_PAD, tm), lambda i, j: (i, 0, 0, j)),
                   pl.BlockSpec((1, tm, QK_PAD), lambda i, j: (i, j, 0)),
                   pl.BlockSpec((1, 1, VT_ROWS, tm), lambda i, j: (i, j, 0, 0)),
                   pl.BlockSpec((1, tm, xaw), lambda i, j: (i, j, 0))),
        compiler_params=_params("arbitrary", "arbitrary"),
        name="mla_proj",
    )(x, w["norm"], w["wa"], w["qn"], w["kvn"], w["wuq"], w["wuk"], cosp, sinp, memk, memv)


def _attn_kernel(qt_ref, k_ref, vt_ref, o_ref, acc_ref, m_ref, s0_ref, s1_ref, cm0_ref, cm1_ref,
                 *, tk, n_chunks):
    heads = range(MLA_HEADS)
    slots = ((s0_ref, cm0_ref), (s1_ref, cm1_ref))
    acc_ref[...] = jnp.zeros(acc_ref.shape, F32)
    m_ref[...] = jnp.full(m_ref.shape, -jnp.inf, F32)

    def scores(c, slot):
        s_ref, cm_ref = slots[slot]
        k = k_ref[0, pl.ds(pl.multiple_of(c * tk, tk), tk), :]
        for hd in heads:
            s = _dot(k, qt_ref[0, hd])
            s_ref[hd] = s
            cm_ref[hd] = jnp.max(s, axis=0, keepdims=True)

    def values(c, p):
        vtw = vt_ref.shape[3]
        return sum(_dot(vt_ref[0, c * ATTN_KV_SUB + j], p[j * vtw:(j + 1) * vtw])
                   for j in range(ATTN_KV_SUB))

    def update(c, slot):
        s_ref, cm_ref = slots[slot]
        m_old = [m_ref[hd] for hd in heads]
        m_new = [jnp.maximum(m_old[hd], cm_ref[hd]) for hd in heads]
        ps = [jnp.exp2(s_ref[hd] - m_new[hd]).astype(BF16) for hd in heads]
        pvs = [values(c, ps[hd]) for hd in heads]
        for hd in heads:
            acc_ref[hd] = jnp.exp2(m_old[hd] - m_new[hd]) * acc_ref[hd] + pvs[hd]
            m_ref[hd] = m_new[hd]

    def fused(c_sc, slot_sc, c_up, slot_up):
        s_ref, cm_ref = slots[slot_sc]
        u_ref, ucm_ref = slots[slot_up]
        k = k_ref[0, pl.ds(pl.multiple_of(c_sc * tk, tk), tk), :]
        for hd in heads:
            s = _dot(k, qt_ref[0, hd])
            s_ref[hd] = s
            cm_ref[hd] = jnp.max(s, axis=0, keepdims=True)
            m_old = m_ref[hd]
            m_new = jnp.maximum(m_old, ucm_ref[hd])
            p = jnp.exp2(u_ref[hd] - m_new).astype(BF16)
            acc_ref[hd] = jnp.exp2(m_old - m_new) * acc_ref[hd] + values(c_up, p)
            m_ref[hd] = m_new

    def steps(c, count):
        for t in range(count):
            fused(c + t + 1, (t + 1) % 2, c + t, t % 2)

    def body(i, carry):
        steps(ATTN_UNROLL * i, ATTN_UNROLL)
        return carry

    scores(0, 0)
    lax.fori_loop(0, n_chunks // ATTN_UNROLL - 1, body, 0)
    steps(n_chunks - ATTN_UNROLL, ATTN_UNROLL - 1)
    update(n_chunks - 1, 1)
    for hd in heads:
        acc = acc_ref[hd]
        o = acc[:KV_LORA] / acc[KV_LORA:KV_LORA + 1]
        o_ref[0, :, hd * KV_LORA:(hd + 1) * KV_LORA] = o.T.astype(BF16)


def _attention(qt, k, vt):
    b, nh, dk, s = qt.shape
    tq = ATTN_Q_TILE
    n_vt, vt_rows, vtw = vt.shape[1:]
    tk = vtw * ATTN_KV_SUB
    n_chunks = s // tk
    assert n_chunks % ATTN_UNROLL == 0, n_chunks
    return pl.pallas_call(
        functools.partial(_attn_kernel, tk=tk, n_chunks=n_chunks),
        out_shape=jax.ShapeDtypeStruct((b, s, nh * KV_LORA), BF16),
        grid=(b, s // tq),
        in_specs=[pl.BlockSpec((1, nh, dk, tq), lambda i, j: (i, 0, 0, j)),
                  pl.BlockSpec((1, s, dk), lambda i, j: (i, 0, 0), pipeline_mode=pl.Buffered(1)),
                  pl.BlockSpec((1, n_vt, vt_rows, vtw), lambda i, j: (i, 0, 0, 0),
                               pipeline_mode=pl.Buffered(1))],
        out_specs=pl.BlockSpec((1, tq, nh * KV_LORA), lambda i, j: (i, j, 0)),
        scratch_shapes=[pltpu.VMEM((nh, vt_rows, tq), F32), pltpu.VMEM((nh, 1, tq), F32),
                        pltpu.VMEM((nh, tk, tq), F32), pltpu.VMEM((nh, tk, tq), F32),
                        pltpu.VMEM((nh, 1, tq), F32), pltpu.VMEM((nh, 1, tq), F32)],
        compiler_params=_params("arbitrary", "arbitrary"),
        name="mla_attention",
    )(qt, k, vt)


def _mla_out_kernel(x_ref, ol_ref, xa_ref, wuv_ref, woo_ref, wox_ref, o_ref, o_scr):
    for pair in range(MLA_HEADS // 2):
        sl = slice(pair * 2 * KV_LORA, (pair + 1) * 2 * KV_LORA)
        o_scr[:, sl] = _dot(ol_ref[0, :, sl], wuv_ref[pair]).astype(BF16)
    o_ref[0] = x_ref[0] + _dot(o_scr[...], woo_ref[...]) + _dot(xa_ref[0], wox_ref[...])


def _mla_out(x, o_lat, xa, w):
    b, s, d = x.shape
    tm = TOKEN_TILE
    ow = o_lat.shape[2]
    xaw = xa.shape[2]
    return pl.pallas_call(
        _mla_out_kernel,
        out_shape=jax.ShapeDtypeStruct((b, s, d), F32),
        grid=(b, s // tm),
        in_specs=[pl.BlockSpec((1, tm, d), lambda i, j: (i, j, 0)),
                  pl.BlockSpec((1, tm, ow), lambda i, j: (i, j, 0)),
                  pl.BlockSpec((1, tm, xaw), lambda i, j: (i, j, 0)),
                  _const_spec(w["wuv"].shape), _const_spec(w["woo"].shape),
                  _const_spec(w["wox"].shape)],
        out_specs=pl.BlockSpec((1, tm, d), lambda i, j: (i, j, 0)),
        scratch_shapes=[pltpu.VMEM((tm, MLA_HEADS * V_HEAD), BF16)],
        compiler_params=_params("arbitrary", "arbitrary"),
        name="mla_out",
    )(x, o_lat, xa, w["wuv"], w["woo"], w["wox"])


def _block_diag_pairs(w):
    n, a, b = w.shape
    z = jnp.zeros((n // 2, a, b), w.dtype)
    top = jnp.concatenate([w[0::2], z], axis=2)
    bot = jnp.concatenate([z, w[1::2]], axis=2)
    return jnp.concatenate([top, bot], axis=1)


def _rope_cols(w1, w2):
    z = jnp.zeros(w1.shape[:-1] + (128 - QK_ROPE,), w1.dtype)
    return jnp.concatenate([w1, w2, z], axis=-1), jnp.concatenate([w2, w1, z], axis=-1)


def _prep_sgu(j, p):
    sw = SGU_GROUPS * SGU_GROUP_DIM
    bs = jnp.broadcast_to(p["sgu_b_s"][j].T[:, :, None], (CHUNK, SGU_GROUPS, SGU_GROUP_DIM))
    return {
        "win": p["sgu_w_in"][j].astype(BF16),
        "vn": p["sgu_v_norm"][j][None, :],
        "ws": p["sgu_w_s"][j].astype(BF16),
        "bs": bs.reshape(CHUNK, sw),
        "wout": p["sgu_w_out"][j].astype(BF16),
    }


def _prep_mla(j, p):
    w_in = p["mla_w_in"][j]
    o1 = Q_LORA
    o2 = o1 + KV_LORA
    o3 = o2 + QK_ROPE
    kra, krb = _rope_cols(w_in[:, o2:o2 + HALF_ROPE], w_in[:, o2 + HALF_ROPE:o3])
    wa = jnp.concatenate([w_in[:, :o2], kra, krb, w_in[:, o3:]], axis=1)
    wuq = p["mla_w_uq"][j].reshape(Q_LORA, MLA_HEADS, QK_NOPE + QK_ROPE)
    qa, qb = _rope_cols(wuq[:, :, QK_NOPE:QK_NOPE + HALF_ROPE], wuq[:, :, QK_NOPE + HALF_ROPE:])
    wuq_all = jnp.concatenate([wuq[:, :, :QK_NOPE].reshape(Q_LORA, -1),
                               qa.reshape(Q_LORA, -1), qb.reshape(Q_LORA, -1)], axis=1)
    wuk = jnp.transpose(p["mla_w_uk"][j], (1, 2, 0))
    wuv = jnp.transpose(p["mla_w_uv"][j], (1, 0, 2))
    w_out = p["mla_w_out"][j]
    ow = MLA_HEADS * V_HEAD
    return {
        "wa": wa.astype(BF16),
        "qn": p["mla_q_norm"][j][None, :],
        "kvn": p["mla_kv_norm"][j][None, :],
        "wuq": wuq_all.astype(BF16),
        "wuk": _block_diag_pairs(wuk).astype(BF16),
        "wuv": _block_diag_pairs(wuv).astype(BF16),
        "woo": w_out[:ow].astype(BF16),
        "wox": w_out[ow:].astype(BF16),
    }


def _rope_tables(seq_len):
    inv_freq = 1.0 / (ROPE_THETA ** (jnp.arange(0, QK_ROPE, 2, dtype=F32) / QK_ROPE))
    ang = jnp.arange(seq_len, dtype=F32)[:, None] * inv_freq[None, :]
    cos, sin = jnp.cos(ang), jnp.sin(ang)
    z = jnp.zeros((seq_len, 128 - QK_ROPE), F32)
    return (jnp.concatenate([cos, cos, z], axis=1), jnp.concatenate([-sin, sin, z], axis=1))


def _trunk(x, mem, p, prepped):
    b, s, d = x.shape
    depth = p["ffn1_w_in"].shape[0]
    memk, memv = _memkv(mem, p["mem_norm"][:, None, :], prepped["w_mem_kv"])
    cosp, sinp = _rope_tables(s)
    for i in range(depth):
        j = i // 2
        x = _ffn(x.reshape(b * s, d), p["ffn1_norm"][i][None, :],
                 prepped["ffn1_w_in"][i], prepped["ffn1_w_out"][i]).reshape(b, s, d)
        if i % 2 == 0:
            w = dict(prepped["sgu"][j], norm=p["mix_norm"][i][None, :])
            x = _sgu(x, w, memk, memv, i)
        else:
            w = dict(prepped["mla"][j], norm=p["mix_norm"][i][None, :])
            q, k, vt, xa = _mla_proj(x, w, cosp, sinp, memk, memv, i)
            o_lat = _attention(q, k, vt)
            x = _mla_out(x, o_lat, xa, w)
        final_g = p["final_norm"][None, :] if i == depth - 1 else None
        x = _ffn(x.reshape(b * s, d), p["ffn2_norm"][i][None, :],
                 prepped["ffn2_w_in"][i], prepped["ffn2_w_out"][i], final_g).reshape(b, s, d)
    return x


def _prepare(p):
    return {
        "ffn1_w_in": p["ffn1_w_in"].astype(BF16), "ffn1_w_out": p["ffn1_w_out"].astype(BF16),
        "ffn2_w_in": p["ffn2_w_in"].astype(BF16), "ffn2_w_out": p["ffn2_w_out"].astype(BF16),
        "w_mem_kv": p["w_mem_kv"].astype(BF16),
        "sgu": [_prep_sgu(j, p) for j in range(p["sgu_w_in"].shape[0])],
        "mla": [_prep_mla(j, p) for j in range(p["mla_w_in"].shape[0])],
    }


def kernel(x_prompt, x_sample, mem_prompt, mem_sample, ffn1_norm, ffn1_w_in, ffn1_w_out, mix_norm, mem_norm, w_mem_kv, ffn2_norm, ffn2_w_in, ffn2_w_out, sgu_w_in, sgu_v_norm, sgu_w_s, sgu_b_s, sgu_w_out, mla_w_in, mla_q_norm, mla_w_uq, mla_kv_norm, mla_w_uk, mla_w_uv, mla_w_out, final_norm):
    p = {
        'ffn1_norm': ffn1_norm, 'ffn1_w_in': ffn1_w_in, 'ffn1_w_out': ffn1_w_out,
        'mix_norm': mix_norm, 'mem_norm': mem_norm, 'w_mem_kv': w_mem_kv,
        'ffn2_norm': ffn2_norm, 'ffn2_w_in': ffn2_w_in, 'ffn2_w_out': ffn2_w_out,
        'sgu_w_in': sgu_w_in, 'sgu_v_norm': sgu_v_norm, 'sgu_w_s': sgu_w_s,
        'sgu_b_s': sgu_b_s, 'sgu_w_out': sgu_w_out,
        'mla_w_in': mla_w_in, 'mla_q_norm': mla_q_norm, 'mla_w_uq': mla_w_uq,
        'mla_kv_norm': mla_kv_norm, 'mla_w_uk': mla_w_uk, 'mla_w_uv': mla_w_uv,
        'mla_w_out': mla_w_out, 'final_norm': final_norm,
    }
    prepped = _prepare(p)
    return (_trunk(x_prompt, mem_prompt, p, prepped), _trunk(x_sample, mem_sample, p, prepped))
```

```python
import functools

import jax
import jax.numpy as jnp
from jax import lax
from jax.experimental import pallas as pl
from jax.experimental.pallas import tpu as pltpu

F32 = jnp.float32
BF16 = jnp.bfloat16

NORM_EPS = 1e-6
ROPE_THETA = 10000.0
XA_HEADS = 4
XA_HEAD_DIM = 128
SGU_GROUPS = 8
SGU_GROUP_DIM = 192
CHUNK = 128
MLA_HEADS = 8
Q_LORA = 256
KV_LORA = 128
QK_NOPE = 128
QK_ROPE = 64
HALF_ROPE = QK_ROPE // 2
V_HEAD = 128
QK_PAD = 256
VT_ROWS = KV_LORA + 16
LOG2_E = 1.4426950408889634

VMEM_LIMIT_BYTES = 56 * 1024 * 1024
TOKEN_TILE = 512
ATTN_Q_TILE = 256
ATTN_KV_SUB = 2
ATTN_UNROLL = 4
ATTN_MIN_TRIPS = 3


def _params(*semantics):
    return pltpu.CompilerParams(dimension_semantics=semantics,
                                vmem_limit_bytes=VMEM_LIMIT_BYTES)


def _const_spec(shape):
    nd = len(shape)
    return pl.BlockSpec(shape, lambda *_: (0,) * nd, pipeline_mode=pl.Buffered(1))


def _rms(x, g):
    return x * lax.rsqrt(jnp.mean(x * x, axis=-1, keepdims=True) + NORM_EPS) * g


def _gelu_tanh(x):
    return 0.5 * x * (1.0 + jnp.tanh(0.7978845608028654 * (x + 0.044715 * (x * x * x))))


def _dot(a, b):
    return jnp.dot(a, b, preferred_element_type=F32)


def _dot_nt(a, b):
    return lax.dot_general(a, b, (((1,), (1,)), ((), ())), preferred_element_type=F32)


def _ffn_kernel(x_ref, g_ref, win_ref, wout_ref, *rest, d_ff, final):
    if final:
        fg_ref, o_ref = rest
    else:
        (o_ref,) = rest
    x = x_ref[...]
    h = _rms(x, g_ref[...]).astype(BF16)
    gu = _dot(h, win_ref[...])
    gate = gu[:, :d_ff]
    up = gu[:, d_ff:]
    act = (gate * (1.0 / (1.0 + jnp.exp(-gate))) * up).astype(BF16)
    y = x + 0.5 * _dot(act, wout_ref[...])
    if final:
        y = _rms(y, fg_ref[...])
    o_ref[...] = y


def _ffn(x2, g, w_in, w_out, final_g=None):
    n, d = x2.shape
    d_ff = w_out.shape[0]
    tm = TOKEN_TILE
    final = final_g is not None
    in_specs = [pl.BlockSpec((tm, d), lambda i: (i, 0)),
                _const_spec((1, d)), _const_spec(w_in.shape), _const_spec(w_out.shape)]
    args = [x2, g, w_in, w_out]
    if final:
        in_specs.append(_const_spec((1, d)))
        args.append(final_g)
    return pl.pallas_call(
        functools.partial(_ffn_kernel, d_ff=d_ff, final=final),
        out_shape=jax.ShapeDtypeStruct((n, d), F32),
        grid=(n // tm,),
        in_specs=in_specs,
        out_specs=pl.BlockSpec((tm, d), lambda i: (i, 0)),
        compiler_params=_params("arbitrary"),
        name="ffn_final" if final else "ffn",
    )(*args)


def _memkv_kernel(mem_ref, g_ref, w_ref, kt_ref, v_ref, *, width):
    h = _rms(mem_ref[0], g_ref[0]).astype(BF16)
    kv = _dot(h, w_ref[0])
    kt_ref[0, 0] = kv[:, :width].T.astype(BF16)
    v_ref[0, 0] = kv[:, width:].astype(BF16)


def _memkv(mem, g, w):
    b, m, d = mem.shape
    nl = w.shape[0]
    width = w.shape[2] // 2
    return pl.pallas_call(
        functools.partial(_memkv_kernel, width=width),
        out_shape=(jax.ShapeDtypeStruct((nl, b, width, m), BF16),
                   jax.ShapeDtypeStruct((nl, b, m, width), BF16)),
        grid=(nl, b),
        in_specs=[pl.BlockSpec((1, m, d), lambda l, i: (i, 0, 0)),
                  pl.BlockSpec((1, 1, d), lambda l, i: (l, 0, 0)),
                  pl.BlockSpec((1, d, 2 * width), lambda l, i: (l, 0, 0))],
        out_specs=(pl.BlockSpec((1, 1, width, m), lambda l, i: (l, i, 0, 0)),
                   pl.BlockSpec((1, 1, m, width), lambda l, i: (l, i, 0, 0))),
        compiler_params=_params("arbitrary", "arbitrary"),
        name="memkv",
    )(mem, g, w)


def _memory_attention(qm, mkt_ref, mv_ref):
    outs = []
    for hh in range(XA_HEADS):
        sl = slice(hh * XA_HEAD_DIM, (hh + 1) * XA_HEAD_DIM)
        s = _dot(qm[:, sl].astype(BF16), mkt_ref[0, 0, sl, :])
        p = jnp.exp(s - jnp.max(s, axis=-1, keepdims=True))
        l = jnp.sum(p, axis=-1, keepdims=True)
        outs.append(_dot(p.astype(BF16), mv_ref[0, 0, :, sl]) / l)
    return outs


def _sgu_kernel(x_ref, g_ref, win_ref, vn_ref, ws_ref, bs_ref, mk_ref, mv_ref, wout_ref,
                o_ref, cat_ref, *, tm):
    sw = SGU_GROUPS * SGU_GROUP_DIM
    pair_w = 2 * SGU_GROUP_DIM
    x = x_ref[0]
    h = _rms(x, g_ref[...]).astype(BF16)
    proj = _dot(h, win_ref[...])
    v = _gelu_tanh(proj[:, sw:2 * sw])
    ms = jnp.mean(v * v, axis=-1, keepdims=True)
    vn = (v * lax.rsqrt(ms + NORM_EPS) * vn_ref[...]).astype(BF16)
    first_half = lax.broadcasted_iota(jnp.int32, (CHUNK, 128), 1) < (SGU_GROUP_DIM - 128)
    for n in range(tm // CHUNK):
        rows = slice(n * CHUNK, (n + 1) * CHUNK)
        for j in range(SGU_GROUPS // 2):
            c0 = j * pair_w
            r0 = _dot(ws_ref[2 * j], vn[rows, c0:c0 + 256])
            r1 = _dot(ws_ref[2 * j + 1], vn[rows, c0 + 128:c0 + 384])
            mixed = (r0[:, :128], jnp.where(first_half, r0[:, 128:], r1[:, :128]), r1[:, 128:])
            for t in range(3):
                cols = slice(c0 + t * 128, c0 + (t + 1) * 128)
                u = _gelu_tanh(proj[rows, cols])
                cat_ref[rows, cols] = (u * (mixed[t] + bs_ref[:, cols])).astype(BF16)
    qm = proj[:, 2 * sw:] * (XA_HEAD_DIM ** -0.5)
    for hh, oh in enumerate(_memory_attention(qm, mk_ref, mv_ref)):
        cat_ref[:, sw + hh * XA_HEAD_DIM:sw + (hh + 1) * XA_HEAD_DIM] = oh.astype(BF16)
    o_ref[0] = x + _dot(cat_ref[...], wout_ref[...])


def _sgu(x, w, memk, memv, layer):
    b, s, d = x.shape
    tm = TOKEN_TILE
    sw = SGU_GROUPS * SGU_GROUP_DIM
    xaw = XA_HEADS * XA_HEAD_DIM
    m = memv.shape[2]
    memt_spec = pl.BlockSpec((1, 1, xaw, m), lambda i, j: (layer, i, 0, 0))
    mem_spec = pl.BlockSpec((1, 1, m, xaw), lambda i, j: (layer, i, 0, 0))
    return pl.pallas_call(
        functools.partial(_sgu_kernel, tm=tm),
        out_shape=jax.ShapeDtypeStruct((b, s, d), F32),
        grid=(b, s // tm),
        in_specs=[pl.BlockSpec((1, tm, d), lambda i, j: (i, j, 0)),
                  _const_spec((1, d)), _const_spec(w["win"].shape), _const_spec((1, sw)),
                  _const_spec(w["ws"].shape), _const_spec(w["bs"].shape),
                  memt_spec, mem_spec, _const_spec(w["wout"].shape)],
        out_specs=pl.BlockSpec((1, tm, d), lambda i, j: (i, j, 0)),
        scratch_shapes=[pltpu.VMEM((tm, sw + xaw), BF16)],
        compiler_params=_params("arbitrary", "arbitrary"),
        name="sgu_mixer",
    )(x, w["norm"], w["win"], w["vn"], w["ws"], w["bs"], memk, memv, w["wout"])


def _mla_proj_kernel(x_ref, g_ref, wa_ref, qn_ref, kvn_ref, wuq_ref, wuk_ref, cos_ref, sin_ref,
                     mk_ref, mv_ref, qt_ref, k_ref, vt_ref, xa_ref):
    x = x_ref[0]
    h = _rms(x, g_ref[...]).astype(BF16)
    proj = _dot(h, wa_ref[...])
    o1 = Q_LORA
    o2 = o1 + KV_LORA
    o3 = o2 + 128
    o4 = o3 + 128
    cos = cos_ref[...]
    sin = sin_ref[...]
    c_q = _rms(proj[:, :o1], qn_ref[...]).astype(BF16)
    c_kv = _rms(proj[:, o1:o2], kvn_ref[...])
    k_rope = proj[:, o2:o3] * cos + proj[:, o3:o4] * sin
    k_ref[0, :, :KV_LORA] = c_kv.astype(BF16)
    k_ref[0, :, KV_LORA:] = k_rope.astype(BF16)
    vt_ref[0, 0, :KV_LORA, :] = c_kv.T.astype(BF16)
    vt_ref[0, 0, KV_LORA:, :] = jnp.ones((VT_ROWS - KV_LORA, c_kv.shape[0]), BF16)

    q = _dot(c_q, wuq_ref[...])
    hw = MLA_HEADS * QK_NOPE
    scale = (QK_NOPE + QK_ROPE) ** -0.5 * LOG2_E
    for pair in range(MLA_HEADS // 2):
        sl = slice(pair * 2 * QK_NOPE, (pair + 1) * 2 * QK_NOPE)
        q_lat = _dot(q[:, sl].astype(BF16), wuk_ref[pair])
        for t in range(2):
            hd = pair * 2 + t
            q_rope = q[:, hw + hd * 128:hw + (hd + 1) * 128] * cos \
                + q[:, 2 * hw + hd * 128:2 * hw + (hd + 1) * 128] * sin
            qt_ref[0, hd, :KV_LORA, :] = (q_lat[:, t * KV_LORA:(t + 1) * KV_LORA] * scale).T.astype(BF16)
            qt_ref[0, hd, KV_LORA:, :] = (q_rope * scale).T.astype(BF16)

    qm = proj[:, o4:] * (XA_HEAD_DIM ** -0.5)
    for hh, oh in enumerate(_memory_attention(qm, mk_ref, mv_ref)):
        xa_ref[0, :, hh * XA_HEAD_DIM:(hh + 1) * XA_HEAD_DIM] = oh.astype(BF16)


def _mla_proj(x, w, cosp, sinp, memk, memv, layer):
    b, s, d = x.shape
    tm = TOKEN_TILE
    xaw = XA_HEADS * XA_HEAD_DIM
    m = memv.shape[2]
    memt_spec = pl.BlockSpec((1, 1, xaw, m), lambda i, j: (layer, i, 0, 0))
    mem_spec = pl.BlockSpec((1, 1, m, xaw), lambda i, j: (layer, i, 0, 0))
    rope_spec = pl.BlockSpec((tm, 128), lambda i, j: (j, 0))
    return pl.pallas_call(
        _mla_proj_kernel,
        out_shape=(jax.ShapeDtypeStruct((b, MLA_HEADS, QK_PAD, s), BF16),
                   jax.ShapeDtypeStruct((b, s, QK_PAD), BF16),
                   jax.ShapeDtypeStruct((b, s // tm, VT_ROWS, tm), BF16),
                   jax.ShapeDtypeStruct((b, s, xaw), BF16)),
        grid=(b, s // tm),
        in_specs=[pl.BlockSpec((1, tm, d), lambda i, j: (i, j, 0)),
                  _const_spec((1, d)), _const_spec(w["wa"].shape),
                  _const_spec((1, Q_LORA)), _const_spec((1, KV_LORA)),
                  _const_spec(w["wuq"].shape), _const_spec(w["wuk"].shape),
                  rope_spec, rope_spec, memt_spec, mem_spec],
        out_specs=(pl.BlockSpec((1, MLA_HEADS, QK_PAD, tm), lambda i, j: (i, 0, 0, j)),
                   pl.BlockSpec((1, tm, QK_PAD), lambda i, j: (i, j, 0)),
                   pl.BlockSpec((1, 1, VT_ROWS, tm), lambda i, j: (i, j, 0, 0)),
                   pl.BlockSpec((1, tm, xaw), lambda i, j: (i, j, 0))),
        compiler_params=_params("arbitrary", "arbitrary"),
        name="mla_proj",
    )(x, w["norm"], w["wa"], w["qn"], w["kvn"], w["wuq"], w["wuk"], cosp, sinp, memk, memv)


def _attn_kernel(qt_ref, k_ref, vt_ref, o_ref, acc_ref, m_ref, s0_ref, s1_ref, cm0_ref, cm1_ref,
                 *, tk, n_chunks, unroll):
    heads = range(MLA_HEADS)
    slots = ((s0_ref, cm0_ref), (s1_ref, cm1_ref))
    acc_ref[...] = jnp.zeros(acc_ref.shape, F32)
    m_ref[...] = jnp.full(m_ref.shape, -jnp.inf, F32)

    def scores(c, slot):
        s_ref, cm_ref = slots[slot]
        k = k_ref[0, pl.ds(pl.multiple_of(c * tk, tk), tk), :]
        for hd in heads:
            s = _dot(k, qt_ref[0, hd])
            s_ref[hd] = s
            cm_ref[hd] = jnp.max(s, axis=0, keepdims=True)

    def values(c, p):
        vtw = vt_ref.shape[3]
        return sum(_dot(vt_ref[0, c * ATTN_KV_SUB + j], p[j * vtw:(j + 1) * vtw])
                   for j in range(ATTN_KV_SUB))

    def update(c, slot):
        s_ref, cm_ref = slots[slot]
        m_old = [m_ref[hd] for hd in heads]
        m_new = [jnp.maximum(m_old[hd], cm_ref[hd]) for hd in heads]
        ps = [jnp.exp2(s_ref[hd] - m_new[hd]).astype(BF16) for hd in heads]
        pvs = [values(c, ps[hd]) for hd in heads]
        for hd in heads:
            acc_ref[hd] = jnp.exp2(m_old[hd] - m_new[hd]) * acc_ref[hd] + pvs[hd]
            m_ref[hd] = m_new[hd]

    def fused(c_sc, slot_sc, c_up, slot_up):
        s_ref, cm_ref = slots[slot_sc]
        u_ref, ucm_ref = slots[slot_up]
        k = k_ref[0, pl.ds(pl.multiple_of(c_sc * tk, tk), tk), :]
        for hd in heads:
            s = _dot(k, qt_ref[0, hd])
            s_ref[hd] = s
            cm_ref[hd] = jnp.max(s, axis=0, keepdims=True)
            m_old = m_ref[hd]
            m_new = jnp.maximum(m_old, ucm_ref[hd])
            p = jnp.exp2(u_ref[hd] - m_new).astype(BF16)
            acc_ref[hd] = jnp.exp2(m_old - m_new) * acc_ref[hd] + values(c_up, p)
            m_ref[hd] = m_new

    def steps(c, count):
        for t in range(count):
            fused(c + t + 1, (t + 1) % 2, c + t, t % 2)

    def body(i, carry):
        steps(unroll * i, unroll)
        return carry

    scores(0, 0)
    lax.fori_loop(0, n_chunks // unroll - 1, body, 0)
    steps(n_chunks - unroll, unroll - 1)
    update(n_chunks - 1, 1)
    for hd in heads:
        acc = acc_ref[hd]
        o = acc[:KV_LORA] / acc[KV_LORA:KV_LORA + 1]
        o_ref[0, :, hd * KV_LORA:(hd + 1) * KV_LORA] = o.T.astype(BF16)


def _attention(qt, k, vt):
    b, nh, dk, s = qt.shape
    tq = ATTN_Q_TILE
    n_vt, vt_rows, vtw = vt.shape[1:]
    tk = vtw * ATTN_KV_SUB
    n_chunks = s // tk
    unroll = ATTN_UNROLL if n_chunks // ATTN_UNROLL - 1 >= ATTN_MIN_TRIPS else 2
    assert n_chunks % unroll == 0, (n_chunks, unroll)
    return pl.pallas_call(
        functools.partial(_attn_kernel, tk=tk, n_chunks=n_chunks, unroll=unroll),
        out_shape=jax.ShapeDtypeStruct((b, s, nh * KV_LORA), BF16),
        grid=(b, s // tq),
        in_specs=[pl.BlockSpec((1, nh, dk, tq), lambda i, j: (i, 0, 0, j)),
                  pl.BlockSpec((1, s, dk), lambda i, j: (i, 0, 0), pipeline_mode=pl.Buffered(1)),
                  pl.BlockSpec((1, n_vt, vt_rows, vtw), lambda i, j: (i, 0, 0, 0),
                               pipeline_mode=pl.Buffered(1))],
        out_specs=pl.BlockSpec((1, tq, nh * KV_LORA), lambda i, j: (i, j, 0)),
        scratch_shapes=[pltpu.VMEM((nh, vt_rows, tq), F32), pltpu.VMEM((nh, 1, tq), F32),
                        pltpu.VMEM((nh, tk, tq), F32), pltpu.VMEM((nh, tk, tq), F32),
                        pltpu.VMEM((nh, 1, tq), F32), pltpu.VMEM((nh, 1, tq), F32)],
        compiler_params=_params("arbitrary", "arbitrary"),
        name="mla_attention",
    )(qt, k, vt)


def _mla_out_kernel(x_ref, ol_ref, xa_ref, wuv_ref, woo_ref, wox_ref, o_ref, o_scr):
    for pair in range(MLA_HEADS // 2):
        sl = slice(pair * 2 * KV_LORA, (pair + 1) * 2 * KV_LORA)
        o_scr[:, sl] = _dot(ol_ref[0, :, sl], wuv_ref[pair]).astype(BF16)
    o_ref[0] = x_ref[0] + _dot(o_scr[...], woo_ref[...]) + _dot(xa_ref[0], wox_ref[...])


def _mla_out(x, o_lat, xa, w):
    b, s, d = x.shape
    tm = TOKEN_TILE
    ow = o_lat.shape[2]
    xaw = xa.shape[2]
    return pl.pallas_call(
        _mla_out_kernel,
        out_shape=jax.ShapeDtypeStruct((b, s, d), F32),
        grid=(b, s // tm),
        in_specs=[pl.BlockSpec((1, tm, d), lambda i, j: (i, j, 0)),
                  pl.BlockSpec((1, tm, ow), lambda i, j: (i, j, 0)),
                  pl.BlockSpec((1, tm, xaw), lambda i, j: (i, j, 0)),
                  _const_spec(w["wuv"].shape), _const_spec(w["woo"].shape),
                  _const_spec(w["wox"].shape)],
        out_specs=pl.BlockSpec((1, tm, d), lambda i, j: (i, j, 0)),
        scratch_shapes=[pltpu.VMEM((tm, MLA_HEADS * V_HEAD), BF16)],
        compiler_params=_params("arbitrary", "arbitrary"),
        name="mla_out",
    )(x, o_lat, xa, w["wuv"], w["woo"], w["wox"])


def _block_diag_pairs(w):
    n, a, b = w.shape
    z = jnp.zeros((n // 2, a, b), w.dtype)
    top = jnp.concatenate([w[0::2], z], axis=2)
    bot = jnp.concatenate([z, w[1::2]], axis=2)
    return jnp.concatenate([top, bot], axis=1)


def _rope_cols(w1, w2):
    z = jnp.zeros(w1.shape[:-1] + (128 - QK_ROPE,), w1.dtype)
    return jnp.concatenate([w1, w2, z], axis=-1), jnp.concatenate([w2, w1, z], axis=-1)


def _prep_sgu(j, p):
    sw = SGU_GROUPS * SGU_GROUP_DIM
    bs = jnp.broadcast_to(p["sgu_b_s"][j].T[:, :, None], (CHUNK, SGU_GROUPS, SGU_GROUP_DIM))
    return {
        "win": p["sgu_w_in"][j].astype(BF16),
        "vn": p["sgu_v_norm"][j][None, :],
        "ws": p["sgu_w_s"][j].astype(BF16),
        "bs": bs.reshape(CHUNK, sw),
        "wout": p["sgu_w_out"][j].astype(BF16),
    }


def _prep_mla(j, p):
    w_in = p["mla_w_in"][j]
    o1 = Q_LORA
    o2 = o1 + KV_LORA
    o3 = o2 + QK_ROPE
    kra, krb = _rope_cols(w_in[:, o2:o2 + HALF_ROPE], w_in[:, o2 + HALF_ROPE:o3])
    wa = jnp.concatenate([w_in[:, :o2], kra, krb, w_in[:, o3:]], axis=1)
    wuq = p["mla_w_uq"][j].reshape(Q_LORA, MLA_HEADS, QK_NOPE + QK_ROPE)
    qa, qb = _rope_cols(wuq[:, :, QK_NOPE:QK_NOPE + HALF_ROPE], wuq[:, :, QK_NOPE + HALF_ROPE:])
    wuq_all = jnp.concatenate([wuq[:, :, :QK_NOPE].reshape(Q_LORA, -1),
                               qa.reshape(Q_LORA, -1), qb.reshape(Q_LORA, -1)], axis=1)
    wuk = jnp.transpose(p["mla_w_uk"][j], (1, 2, 0))
    wuv = jnp.transpose(p["mla_w_uv"][j], (1, 0, 2))
    w_out = p["mla_w_out"][j]
    ow = MLA_HEADS * V_HEAD
    return {
        "wa": wa.astype(BF16),
        "qn": p["mla_q_norm"][j][None, :],
        "kvn": p["mla_kv_norm"][j][None, :],
        "wuq": wuq_all.astype(BF16),
        "wuk": _block_diag_pairs(wuk).astype(BF16),
        "wuv": _block_diag_pairs(wuv).astype(BF16),
        "woo": w_out[:ow].astype(BF16),
        "wox": w_out[ow:].astype(BF16),
    }


def _rope_tables(seq_len):
    inv_freq = 1.0 / (ROPE_THETA ** (jnp.arange(0, QK_ROPE, 2, dtype=F32) / QK_ROPE))
    ang = jnp.arange(seq_len, dtype=F32)[:, None] * inv_freq[None, :]
    cos, sin = jnp.cos(ang), jnp.sin(ang)
    z = jnp.zeros((seq_len, 128 - QK_ROPE), F32)
    return (jnp.concatenate([cos, cos, z], axis=1), jnp.concatenate([-sin, sin, z], axis=1))


def _trunk(x, mem, p, prepped):
    b, s, d = x.shape
    depth = p["ffn1_w_in"].shape[0]
    memk, memv = _memkv(mem, p["mem_norm"][:, None, :], prepped["w_mem_kv"])
    cosp, sinp = _rope_tables(s)
    for i in range(depth):
        j = i // 2
        x = _ffn(x.reshape(b * s, d), p["ffn1_norm"][i][None, :],
                 prepped["ffn1_w_in"][i], prepped["ffn1_w_out"][i]).reshape(b, s, d)
        if i % 2 == 0:
            w = dict(prepped["sgu"][j], norm=p["mix_norm"][i][None, :])
            x = _sgu(x, w, memk, memv, i)
        else:
            w = dict(prepped["mla"][j], norm=p["mix_norm"][i][None, :])
            q, k, vt, xa = _mla_proj(x, w, cosp, sinp, memk, memv, i)
            o_lat = _attention(q, k, vt)
            x = _mla_out(x, o_lat, xa, w)
        final_g = p["final_norm"][None, :] if i == depth - 1 else None
        x = _ffn(x.reshape(b * s, d), p["ffn2_norm"][i][None, :],
                 prepped["ffn2_w_in"][i], prepped["ffn2_w_out"][i], final_g).reshape(b, s, d)
    return x


def _prepare(p):
    return {
        "ffn1_w_in": p["ffn1_w_in"].astype(BF16), "ffn1_w_out": p["ffn1_w_out"].astype(BF16),
        "ffn2_w_in": p["ffn2_w_in"].astype(BF16), "ffn2_w_out": p["ffn2_w_out"].astype(BF16),
        "w_mem_kv": p["w_mem_kv"].astype(BF16),
        "sgu": [_prep_sgu(j, p) for j in range(p["sgu_w_in"].shape[0])],
        "mla": [_prep_mla(j, p) for j in range(p["mla_w_in"].shape[0])],
    }


def kernel(x_prompt, x_sample, mem_prompt, mem_sample, ffn1_norm, ffn1_w_in, ffn1_w_out, mix_norm, mem_norm, w_mem_kv, ffn2_norm, ffn2_w_in, ffn2_w_out, sgu_w_in, sgu_v_norm, sgu_w_s, sgu_b_s, sgu_w_out, mla_w_in, mla_q_norm, mla_w_uq, mla_kv_norm, mla_w_uk, mla_w_uv, mla_w_out, final_norm):
    p = {
        'ffn1_norm': ffn1_norm, 'ffn1_w_in': ffn1_w_in, 'ffn1_w_out': ffn1_w_out,
        'mix_norm': mix_norm, 'mem_norm': mem_norm, 'w_mem_kv': w_mem_kv,
        'ffn2_norm': ffn2_norm, 'ffn2_w_in': ffn2_w_in, 'ffn2_w_out': ffn2_w_out,
        'sgu_w_in': sgu_w_in, 'sgu_v_norm': sgu_v_norm, 'sgu_w_s': sgu_w_s,
        'sgu_b_s': sgu_b_s, 'sgu_w_out': sgu_w_out,
        'mla_w_in': mla_w_in, 'mla_q_norm': mla_q_norm, 'mla_w_uq': mla_w_uq,
        'mla_kv_norm': mla_kv_norm, 'mla_w_uk': mla_w_uk, 'mla_w_uv': mla_w_uv,
        'mla_w_out': mla_w_out, 'final_norm': final_norm,
    }
    prepped = _prepare(p)
    return (_trunk(x_prompt, mem_prompt, p, prepped), _trunk(x_sample, mem_sample, p, prepped))
```
